```python
import math
import jax, jax.numpy as jnp
from jax import lax
import numpy as np

D_MODEL = 1024
BATCH = 8
SEQ = 4096
DEPTH = 4

GRID_W = 64
CTX_LEN = 256
N_EVEN = (DEPTH + 1) // 2
N_ODD = DEPTH // 2
EPS = 1e-6
ROPE_THETA = 10000.0
Q_BLOCK = 128
MOD_STD = 0.5

D_RNN = D_MODEL // 2
LRU_BLOCKS = 8
LRU_BS = D_RNN // LRU_BLOCKS
LRU_CONV_W = 4
LRU_PAD = (1, 2)
RG_C = 8.0
LRU_A_MIN = 0.9
LRU_A_MAX = 0.999

MLA_HEADS = 8
MLA_NOPE = 64
MLA_ROPE = 32
MLA_V = (D_MODEL - D_RNN) // MLA_HEADS
MLA_Q_RANK = 3 * D_MODEL // 8
MLA_KV_RANK = D_MODEL // 4
IN_AB = 2 * D_RNN + MLA_Q_RANK + MLA_KV_RANK + MLA_ROPE

DIFF_HEADS = 8
DIFF_DH = D_MODEL // (2 * DIFF_HEADS)
IN_C = 3 * D_MODEL

D_FF = 2816
FFN_CONV_W = 3
FFN_PAD = (1, 1)

kernel_name = 'hybrid_rglru_mla_diffattn_convffn_dit'


def rmsnorm(x, g):
    xf = x.astype(jnp.float32)
    y = xf * lax.rsqrt(jnp.mean(xf * xf, axis=-1, keepdims=True) + EPS)
    return (y * g.astype(jnp.float32)).astype(x.dtype)


def modulate(u, shift, scale):
    return u * (1 + scale) + shift


def split_cols(z, sizes):
    idx = [int(s) for s in np.cumsum(sizes)[:-1]]
    return jnp.split(z, idx, axis=-1)


def dwconv(x, w, b, pad):
    C = x.shape[-1]
    y = lax.conv_general_dilated(x, w[:, None, :], window_strides=(1,), padding=[pad],
                                 dimension_numbers=('NWC', 'WIO', 'NWC'), feature_group_count=C)
    return y + b


def axial_angles(row, col, rot_dim):
    da = rot_dim // 2
    inv = ROPE_THETA ** (-jnp.arange(0, da, 2, dtype=jnp.float32) / da)
    return row[:, None] * inv, col[:, None] * inv


def rope_1d(x, ang):
    n = ang.shape[-1]
    ang = ang.reshape((1, ang.shape[0]) + (1,) * (x.ndim - 3) + (n,))
    cos = jnp.cos(ang).astype(x.dtype)
    sin = jnp.sin(ang).astype(x.dtype)
    x1, x2 = x[..., :n], x[..., n:]
    return jnp.concatenate([x1 * cos - x2 * sin, x2 * cos + x1 * sin], axis=-1)


def rope_2d(x, ang_row, ang_col):
    da = x.shape[-1] // 2
    return jnp.concatenate([rope_1d(x[..., :da], ang_row), rope_1d(x[..., da:], ang_col)], axis=-1)


def _sweep_queries(fn, q):
    B, S = q.shape[:2]
    nb = S // Q_BLOCK
    qb = jnp.moveaxis(q.reshape((B, nb, Q_BLOCK) + q.shape[2:]), 1, 0)
    out = jnp.moveaxis(lax.map(fn, qb), 0, 1)
    return out.reshape((B, S) + out.shape[3:])


def dense_attention(q, k, v):
    scale = q.shape[-1] ** -0.5
    def blk(qb):
        s = jnp.einsum('bqhd,bkhd->bhqk', qb, k).astype(jnp.float32) * scale
        p = jax.nn.softmax(s, axis=-1).astype(v.dtype)
        return jnp.einsum('bhqk,bkhd->bqhd', p, v)
    return _sweep_queries(blk, q)


def diff_attention(q, k, v, lam):
    scale = DIFF_DH ** -0.5
    def blk(qb):
        s = jnp.einsum('bqhmd,bkhmd->bhmqk', qb, k).astype(jnp.float32) * scale
        p = jax.nn.softmax(s, axis=-1)
        w = (p[:, :, 0] - lam * p[:, :, 1]).astype(v.dtype)
        return jnp.einsum('bhqk,bkhe->bqhe', w, v)
    return _sweep_queries(blk, q)


def rglru_coeffs(x, gate_w, gate_b, lam):
    B, T, _ = x.shape
    xb = x.reshape(B, T, LRU_BLOCKS, LRU_BS)
    pre = jnp.einsum('btnc,gncd->gbtnd', xb, gate_w).reshape(2, B, T, D_RNN) + gate_b[:, None, None, :]
    gates = jax.nn.sigmoid(pre)
    r, i = gates[0], gates[1]
    log_a = -RG_C * r * jax.nn.softplus(-lam)
    a = jnp.exp(log_a)
    b = jnp.sqrt(-jnp.expm1(2.0 * log_a)) * (i * x)
    return a, b


def _combine(left, right):
    a_l, b_l = left
    a_r, b_r = right
    return a_l * a_r, a_r * b_l + b_r


def linear_scan(a, b, h0, reverse):
    A, H = lax.associative_scan(_combine, (a, b), axis=1, reverse=reverse)
    if h0 is not None:
        H = H + A * h0[:, None, :]
    return H


def rglru_direction(x, xc, gate_w, gate_b, lam, reverse):
    a_c, b_c = rglru_coeffs(xc, gate_w, gate_b, lam)
    hc = linear_scan(a_c, b_c, None, reverse)
    h0 = hc[:, 0] if reverse else hc[:, -1]
    a, b = rglru_coeffs(x, gate_w, gate_b, lam)
    return linear_scan(a, b, h0, reverse), hc


def mla_project(q_lat, kv_lat, k_rope, g_q, g_kv, w_uq, w_ukv, gqn, gkn, gqr, gkr, angs):
    B, T, _ = q_lat.shape
    q = (rmsnorm(q_lat, g_q) @ w_uq).reshape(B, T, MLA_HEADS, MLA_NOPE + MLA_ROPE)
    kv = (rmsnorm(kv_lat, g_kv) @ w_ukv).reshape(B, T, MLA_HEADS, MLA_NOPE + MLA_V)
    q_nope = rmsnorm(q[..., :MLA_NOPE], gqn)
    q_rope = rmsnorm(q[..., MLA_NOPE:], gqr)
    k_nope = rmsnorm(kv[..., :MLA_NOPE], gkn)
    v = kv[..., MLA_NOPE:]
    k_rope = rmsnorm(k_rope[:, :, None, :], gkr)
    if angs is not None:
        q_rope = rope_2d(q_rope, *angs)
        k_rope = rope_2d(k_rope, *angs)
    k_rope = jnp.broadcast_to(k_rope, (B, T, MLA_HEADS, MLA_ROPE))
    return (jnp.concatenate([q_nope, q_rope], axis=-1),
            jnp.concatenate([k_nope, k_rope], axis=-1), v)


def even_mixer(u, uc, angs, w_in, conv_w, conv_b, gate_w, gate_b, lam, g_q, g_kv, w_uq, w_ukv,
               gqn, gkn, gqr, gkr, w_out, ctx_out):
    B, T, _ = u.shape
    sizes = (D_RNN, D_RNN, MLA_Q_RANK, MLA_KV_RANK, MLA_ROPE)
    xr, gr, ql, kvl, kr = split_cols(u @ w_in, sizes)
    xrc, grc, qlc, kvlc, krc = split_cols(uc @ w_in, sizes)
    xr = dwconv(xr, conv_w, conv_b, LRU_PAD)
    xrc = dwconv(xrc, conv_w, conv_b, LRU_PAD)
    hf, hfc = rglru_direction(xr, xrc, gate_w[0], gate_b[0], lam[0], False)
    hb, hbc = rglru_direction(xr, xrc, gate_w[1], gate_b[1], lam[1], True)
    y_a = (hf + hb) * jax.nn.gelu(gr)
    q, k, v = mla_project(ql, kvl, kr, g_q, g_kv, w_uq, w_ukv, gqn, gkn, gqr, gkr, angs)
    qc, kc, vc = mla_project(qlc, kvlc, krc, g_q, g_kv, w_uq, w_ukv, gqn, gkn, gqr, gkr, None)
    y_b = dense_attention(q, jnp.concatenate([k, kc], axis=1), jnp.concatenate([v, vc], axis=1))
    y = jnp.concatenate([y_a, y_b.reshape(B, T, -1)], axis=-1) @ w_out
    if not ctx_out:
        return y, None
    Bc, Tc, _ = uc.shape
    y_ac = (hfc + hbc) * jax.nn.gelu(grc)
    y_bc = dense_attention(qc, kc, vc).reshape(Bc, Tc, -1)
    yc = jnp.concatenate([y_ac, y_bc], axis=-1) @ w_out
    return y, yc


def diff_project(u, w_in, gq, gk, angs):
    B, T, _ = u.shape
    q, k, v = jnp.split(u @ w_in, 3, axis=-1)
    q = rmsnorm(q.reshape(B, T, DIFF_HEADS, 2, DIFF_DH), gq)
    k = rmsnorm(k.reshape(B, T, DIFF_HEADS, 2, DIFF_DH), gk)
    v = v.reshape(B, T, DIFF_HEADS, 2 * DIFF_DH)
    if angs is not None:
        q = rope_2d(q, *angs)
        k = rope_2d(k, *angs)
    return q, k, v


def diff_mixer(u, uc, angs, w_in, gq, gk, lam_vecs, g_out, w_out, lam_init, ctx_out):
    q, k, v = diff_project(u, w_in, gq, gk, angs)
    qc, kc, vc = diff_project(uc, w_in, gq, gk, None)
    lf = lam_vecs.astype(jnp.float32)
    lam = jnp.exp(jnp.sum(lf[0] * lf[1])) - jnp.exp(jnp.sum(lf[2] * lf[3])) + lam_init
    def head_out(o):
        B, T = o.shape[:2]
        return (rmsnorm(o, g_out) * (1.0 - lam_init)).reshape(B, T, -1) @ w_out
    y = head_out(diff_attention(q, jnp.concatenate([k, kc], axis=1),
                                jnp.concatenate([v, vc], axis=1), lam))
    if not ctx_out:
        return y, None
    return y, head_out(diff_attention(qc, kc, vc, lam))


def conv_ffn(u, w_up, conv_w, conv_b, w_down):
    a, g = jnp.split(u @ w_up, 2, axis=-1)
    g = dwconv(g, conv_w, conv_b, FFN_PAD)
    return (jax.nn.gelu(g) * a) @ w_down


def setup_inputs(seed: int = 0) -> dict:
    key = jax.random.key(seed)
    ks = iter(jax.random.split(key, 48))
    f32 = jnp.float32
    D = D_MODEL
    NE, NO = N_EVEN, N_ODD
    def nrm(shape, std):
        return std * jax.random.normal(next(ks), shape, f32)
    def gain(shape):
        return 1.0 + nrm(shape, 0.02)
    u = jax.random.uniform(next(ks), (NE, 2, D_RNN), f32, LRU_A_MIN, LRU_A_MAX)
    a0 = u ** (1.0 / RG_C)
    lru_lambda = jnp.log(a0) - jnp.log1p(-a0)
    return {
        'x': nrm((BATCH, SEQ, D), 1.0),
        'c': nrm((BATCH, D), 1.0),
        'ctx': nrm((BATCH, CTX_LEN, D), 1.0),
        'c_ctx': nrm((D,), 1.0),
        'w_mod': nrm((DEPTH, D, 6 * D), MOD_STD * D ** -0.5),
        'b_mod': nrm((DEPTH, 6 * D), 0.01),
        'g_norm1': gain((DEPTH, D)),
        'g_norm2': gain((DEPTH, D)),
        'w_in_ab': nrm((NE, D, IN_AB), D ** -0.5),
        'lru_conv_w': nrm((NE, LRU_CONV_W, D_RNN), LRU_CONV_W ** -0.5),
        'lru_conv_b': nrm((NE, D_RNN), 0.01),
        'lru_gate_w': nrm((NE, 2, 2, LRU_BLOCKS, LRU_BS, LRU_BS), LRU_BS ** -0.5),
        'lru_gate_b': nrm((NE, 2, 2, D_RNN), 0.01),
        'lru_lambda': lru_lambda,
        'mla_g_q': gain((NE, MLA_Q_RANK)),
        'mla_g_kv': gain((NE, MLA_KV_RANK)),
        'mla_w_uq': nrm((NE, MLA_Q_RANK, MLA_HEADS * (MLA_NOPE + MLA_ROPE)), MLA_Q_RANK ** -0.5),
        'mla_w_ukv': nrm((NE, MLA_KV_RANK, MLA_HEADS * (MLA_NOPE + MLA_V)), MLA_KV_RANK ** -0.5),
        'mla_gq_nope': gain((NE, MLA_NOPE)),
        'mla_gk_nope': gain((NE, MLA_NOPE)),
        'mla_gq_rope': gain((NE, MLA_ROPE)),
        'mla_gk_rope': gain((NE, MLA_ROPE)),
        'w_out_ab': nrm((NE, D_RNN + MLA_HEADS * MLA_V, D), D ** -0.5),
        'w_in_c': nrm((NO, D, IN_C), D ** -0.5),
        'diff_gq': gain((NO, DIFF_DH)),
        'diff_gk': gain((NO, DIFF_DH)),
        'diff_lam': nrm((NO, 4, DIFF_DH), 0.1),
        'diff_g_out': gain((NO, 2 * DIFF_DH)),
        'w_out_c': nrm((NO, D, D), D ** -0.5),
        'ffn_w_up': nrm((DEPTH, D, 2 * D_FF), D ** -0.5),
        'ffn_conv_w': nrm((DEPTH, FFN_CONV_W, D_FF), FFN_CONV_W ** -0.5),
        'ffn_conv_b': nrm((DEPTH, D_FF), 0.01),
        'ffn_w_down': nrm((DEPTH, D_FF, D), D_FF ** -0.5),
    }


def reference(x, c, ctx, c_ctx, w_mod, b_mod, g_norm1, g_norm2, w_in_ab, lru_conv_w, lru_conv_b,
              lru_gate_w, lru_gate_b, lru_lambda, mla_g_q, mla_g_kv, mla_w_uq, mla_w_ukv,
              mla_gq_nope, mla_gk_nope, mla_gq_rope, mla_gk_rope, w_out_ab, w_in_c, diff_gq,
              diff_gk, diff_lam, diff_g_out, w_out_c, ffn_w_up, ffn_conv_w, ffn_conv_b, ffn_w_down):
    S = x.shape[1]
    rows = S // GRID_W
    row = jnp.repeat(jnp.arange(rows, dtype=jnp.float32), GRID_W)
    col = jnp.tile(jnp.arange(GRID_W, dtype=jnp.float32), rows)
    ang_mla = axial_angles(row, col, MLA_ROPE)
    ang_diff = axial_angles(row, col, DIFF_DH)
    silu_c = jax.nn.silu(c)
    silu_cc = jax.nn.silu(c_ctx)
    h, hc = x, ctx
    for li in range(DEPTH):
        ctx_out = li < DEPTH - 1
        mod = (silu_c @ w_mod[li] + b_mod[li])[:, None, :]
        mod_c = (silu_cc @ w_mod[li] + b_mod[li])[None, None, :]
        sh1, sc1, ga1, sh2, sc2, ga2 = jnp.split(mod, 6, axis=-1)
        csh1, csc1, cga1, csh2, csc2, cga2 = jnp.split(mod_c, 6, axis=-1)
        u = modulate(rmsnorm(h, g_norm1[li]), sh1, sc1)
        uc = modulate(rmsnorm(hc, g_norm1[li]), csh1, csc1)
        if li % 2 == 0:
            e = li // 2
            y, yc = even_mixer(u, uc, ang_mla, w_in_ab[e], lru_conv_w[e], lru_conv_b[e],
                               lru_gate_w[e], lru_gate_b[e], lru_lambda[e], mla_g_q[e], mla_g_kv[e],
                               mla_w_uq[e], mla_w_ukv[e], mla_gq_nope[e], mla_gk_nope[e],
                               mla_gq_rope[e], mla_gk_rope[e], w_out_ab[e], ctx_out)
        else:
            o = li // 2
            lam_init = 0.8 - 0.6 * math.exp(-0.3 * li)
            y, yc = diff_mixer(u, uc, ang_diff, w_in_c[o], diff_gq[o], diff_gk[o], diff_lam[o],
                               diff_g_out[o], w_out_c[o], lam_init, ctx_out)
        h = h + ga1 * y
        u = modulate(rmsnorm(h, g_norm2[li]), sh2, sc2)
        h = h + ga2 * conv_ffn(u, ffn_w_up[li], ffn_conv_w[li], ffn_conv_b[li], ffn_w_down[li])
        if ctx_out:
            hc = hc + cga1 * yc
            uc = modulate(rmsnorm(hc, g_norm2[li]), csh2, csc2)
            hc = hc + cga2 * conv_ffn(uc, ffn_w_up[li], ffn_conv_w[li], ffn_conv_b[li], ffn_w_down[li])
    return h
```

```python
import functools
import math

import numpy as np
import jax
import jax.numpy as jnp
from jax import lax
from jax.experimental import pallas as pl
from jax.experimental.pallas import tpu as pltpu

F32 = jnp.float32
BF16 = jnp.bfloat16

EPS = 1e-6
ROPE_THETA = 10000.0
GRID_W = 64
RG_C = 8.0
LRU_BLOCKS = 8
MLA_HEADS = 8
MLA_NOPE = 64
MLA_ROPE = 32
DIFF_HEADS = 8

LANES = 128
BF16_ROWS = 16
ROW_TILE = 256
VMEM_LIMIT = 52 * 1024 * 1024


def _bf(x):
    return x.astype(BF16)


def _dot(a, b):
    return jnp.dot(a, b, preferred_element_type=F32)


def _dot_nt(a, b):
    return lax.dot_general(a, b, (((1,), (1,)), ((), ())), preferred_element_type=F32)


def _rms(x, g):
    ms = jnp.mean(x * x, axis=-1, keepdims=True)
    return x * lax.rsqrt(ms + EPS) * g


def _expm1(x):
    u = jnp.exp(x)
    small = jnp.where(u == 1.0, x, (u - 1.0) * x / jnp.log(u))
    return jnp.where(jnp.abs(x) > 1.0, u - 1.0, small)


def _log1p(y):
    w = 1.0 + y
    return jnp.where(w == 1.0, y, jnp.log(w) * y / (w - 1.0))


def _seg_sumsq(x, m):
    x2 = x * x
    hi = _bf(x2)
    lo = _bf(x2 - hi.astype(F32))
    return _dot(hi, m) + _dot(lo, m)


def _heads_norm(x, m, invlen, gain, nheads):
    stacked = jnp.concatenate([x[:, h * LANES:(h + 1) * LANES] for h in range(nheads)], axis=0)
    ss = _seg_sumsq(stacked, m)
    return stacked * lax.rsqrt(ss * invlen + EPS) * gain


def _rope(x, cos, sina, sinb, shift):
    return x * cos + pltpu.roll(x, LANES - shift, 1) * sina + pltpu.roll(x, shift, 1) * sinb


def _const_spec(shape):
    n = len(shape)
    return pl.BlockSpec(shape, lambda *_: (0,) * n, pipeline_mode=pl.Buffered(1))


def _params(sem):
    return pltpu.CompilerParams(dimension_semantics=sem, vmem_limit_bytes=VMEM_LIMIT)


def _mod_kernel(cc_ref, w_ref, b_ref, o_ref):
    x = cc_ref[...]
    sx = x * jax.nn.sigmoid(x)
    o_ref[0] = _dot(_bf(sx), _bf(w_ref[0])) + b_ref[0]


def _modulation(cc, w_mod, b_mod):
    depth, d, d6 = w_mod.shape
    rows = cc.shape[0]
    return pl.pallas_call(
        _mod_kernel,
        grid=(depth, d6 // d),
        in_specs=[pl.BlockSpec((rows, d), lambda l, j: (0, 0)),
                  pl.BlockSpec((1, d, d), lambda l, j: (l, 0, j)),
                  pl.BlockSpec((1, 1, d), lambda l, j: (l, 0, j))],
        out_specs=pl.BlockSpec((1, rows, d), lambda l, j: (l, 0, j)),
        out_shape=jax.ShapeDtypeStruct((depth, rows, d6), F32),
        compiler_params=_params(("arbitrary", "arbitrary")),
        name="modulation",
    )(cc, w_mod, b_mod.reshape(depth, 1, d6))


def _even_in_kernel(h_ref, mod_ref, g1_ref, win_ref, gq_ref, gkv_ref, wuq_ref, wuk_ref, wuv_ref,
                    m_ref, invlen_ref, gainq_ref, gaink_ref, gkr_ref, cos_ref, sina_ref, sinb_ref,
                    xr_ref, gr_ref, q_ref, k_ref, v_ref, *, d_rnn, q_rank, kv_rank):
    tr = h_ref.shape[1]
    mod = mod_ref[0, 0]
    u = _rms(h_ref[0], g1_ref[...]) * (1.0 + mod[1:2]) + mod[0:1]
    z = _dot(_bf(u), win_ref[...])
    o = 2 * d_rnn
    xr_ref[0] = z[:, :d_rnn]
    gr_ref[0] = z[:, d_rnn:o]
    ql = z[:, o:o + q_rank]
    kvl = z[:, o + q_rank:o + q_rank + kv_rank]
    kr = z[:, o + q_rank + kv_rank:]

    cos, sina, sinb = cos_ref[...], sina_ref[...], sinb_ref[...]
    m, invlen = m_ref[...], invlen_ref[...]

    q = _dot(_bf(_rms(ql, gq_ref[...])), wuq_ref[...])
    qn = _heads_norm(q, m, invlen, gainq_ref[...], MLA_HEADS)
    for h in range(MLA_HEADS):
        q_ref[0, :, h * LANES:(h + 1) * LANES] = _bf(
            _rope(qn[h * tr:(h + 1) * tr], cos, sina, sinb, MLA_ROPE // 4))

    kvn = _bf(_rms(kvl, gkv_ref[...]))
    v_ref[0] = _bf(_dot(kvn, wuv_ref[...]))
    kn = _heads_norm(_dot(kvn, wuk_ref[...]), m, invlen, gaink_ref[...], MLA_HEADS)
    ss = jnp.sum(kr * kr, axis=-1, keepdims=True) * (1.0 / MLA_ROPE)
    k_rope = _rope(kr * lax.rsqrt(ss + EPS) * gkr_ref[...], cos, sina, sinb, MLA_ROPE // 4)
    for h in range(MLA_HEADS):
        k_ref[0, :, h * LANES:(h + 1) * LANES] = _bf(kn[h * tr:(h + 1) * tr] + k_rope)


def _even_in(h, modtab, g1, p, rope, nlat):
    b, l, d = h.shape
    tr = ROW_TILE
    d_rnn = p["d_rnn"]
    hl = MLA_HEADS * LANES
    row = lambda w: pl.BlockSpec((1, tr, w), lambda i, t: (i, t, 0))
    tab = pl.BlockSpec((tr, LANES), lambda i, t: (t, 0))
    consts = [g1, p["w_in"], p["g_q"], p["g_kv"], p["w_uq"], p["w_uk"], p["w_uv"],
              p["m"], p["invlen"], p["gainq"], p["gaink"], p["gkr"]]
    kern = functools.partial(_even_in_kernel, d_rnn=d_rnn, q_rank=p["g_q"].shape[1],
                             kv_rank=p["g_kv"].shape[1])
    return pl.pallas_call(
        kern,
        grid=(b, l // tr),
        in_specs=[row(d), pl.BlockSpec((1, 1, 6, d), lambda i, t: (i, t // nlat, 0, 0))]
                 + [_const_spec(c.shape) for c in consts] + [tab, tab, tab],
        out_specs=[row(d_rnn), row(d_rnn), row(hl), row(hl), row(p["w_uv"].shape[1])],
        out_shape=[jax.ShapeDtypeStruct((b, l, d_rnn), F32), jax.ShapeDtypeStruct((b, l, d_rnn), F32),
                   jax.ShapeDtypeStruct((b, l, hl), BF16), jax.ShapeDtypeStruct((b, l, hl), BF16),
                   jax.ShapeDtypeStruct((b, l, p["w_uv"].shape[1]), BF16)],
        compiler_params=_params(("parallel", "parallel")),
        name="even_in",
    )(h, modtab, *consts, *rope)


def _scan_chunk(a, b, reverse, row):
    ch = a.shape[0]
    d = 1
    while d < ch:
        if not reverse:
            if d < 8:
                keep = row >= d
                a_sh = jnp.where(keep, pltpu.roll(a, d, 0), 1.0)
                b_sh = jnp.where(keep, pltpu.roll(b, d, 0), 0.0)
            else:
                a_sh = jnp.concatenate([jnp.ones((d, LANES), F32), a[:ch - d]], axis=0)
                b_sh = jnp.concatenate([jnp.zeros((d, LANES), F32), b[:ch - d]], axis=0)
        else:
            if d < 8:
                keep = row < ch - d
                a_sh = jnp.where(keep, pltpu.roll(a, ch - d, 0), 1.0)
                b_sh = jnp.where(keep, pltpu.roll(b, ch - d, 0), 0.0)
            else:
                a_sh = jnp.concatenate([a[d:], jnp.ones((d, LANES), F32)], axis=0)
                b_sh = jnp.concatenate([b[d:], jnp.zeros((d, LANES), F32)], axis=0)
        b = a * b_sh + b
        a = a * a_sh
        d *= 2
    return a, b


def _lru_kernel(xr_ref, gr_ref, cw_ref, cb_ref, wg_ref, bg_ref, lam_ref, y_ref, xpad, hf, *, s, ch):
    l = xr_ref.shape[1]
    nlat, nctx = s // ch, (l - s) // ch
    conv_w = cw_ref.shape[0]
    ext_rows = ch + 16
    xpad[0:8, :] = jnp.zeros((8, LANES), F32)
    xpad[8:8 + l, :] = xr_ref[0]
    xpad[8 + l:16 + l, :] = jnp.zeros((8, LANES), F32)
    cw, cb = cw_ref[...], cb_ref[...]
    nl = -lam_ref[0]
    sp = jnp.maximum(nl, 0.0) + _log1p(jnp.exp(-jnp.abs(nl)))
    row = lax.broadcasted_iota(jnp.int32, (ch, LANES), 0)

    def chunk(r0, direction, carry, in_ctx):
        ext = xpad[pl.ds(r0, ext_rows), :]
        x = jnp.zeros((ch, LANES), F32) + cb
        for w in range(conv_w):
            xs = pltpu.roll(ext, ext_rows - (7 + w), 0)[0:ch]
            src = r0 + row + (w - 1)
            if in_ctx and w == 0:
                xs = jnp.where(src >= s, xs, 0.0)
            if not in_ctx and w >= 2:
                xs = jnp.where(src < s, xs, 0.0)
            x = x + xs * cw[w:w + 1]
        c0 = direction * 2 * LANES
        pre = _dot(_bf(x), wg_ref[0, :, c0:c0 + 2 * LANES]) + bg_ref[0, :, c0:c0 + 2 * LANES]
        r = jax.nn.sigmoid(pre[:, :LANES])
        i = jax.nn.sigmoid(pre[:, LANES:])
        log_a = -RG_C * r * sp[direction:direction + 1]
        a = jnp.exp(log_a)
        bb = jnp.sqrt(-_expm1(2.0 * log_a)) * (i * x)
        a_c, b_c = _scan_chunk(a, bb, direction == 1, row)
        hh = b_c + a_c * carry
        carry = hh[0:1] if direction == 1 else hh[ch - 1:ch]
        return hh, carry

    carry = jnp.zeros((1, LANES), F32)
    for c in range(nctx):
        r0 = s + c * ch
        hh, carry = chunk(r0, 0, carry, True)
        hf[r0:r0 + ch, :] = hh

    def fwd(i, carry):
        r0 = pl.multiple_of(i * ch, ch)
        hh, carry = chunk(r0, 0, carry, False)
        hf[pl.ds(r0, ch), :] = hh
        return carry

    lax.fori_loop(0, nlat, fwd, carry)

    def emit(r0, hb):
        rows = pl.ds(r0, ch)
        y_ref[0, rows, :] = _bf((hf[rows, :] + hb) * jax.nn.gelu(gr_ref[0, rows, :]))

    carry = jnp.zeros((1, LANES), F32)
    for c in reversed(range(nctx)):
        r0 = s + c * ch
        hb, carry = chunk(r0, 1, carry, True)
        emit(r0, hb)

    def bwd(i, carry):
        r0 = pl.multiple_of((nlat - 1 - i) * ch, ch)
        hb, carry = chunk(r0, 1, carry, False)
        emit(r0, hb)
        return carry

    lax.fori_loop(0, nlat, bwd, carry)


def _lru(xr, gr, p, s):
    b, l, d_rnn = xr.shape
    ng = d_rnn // LANES
    col = pl.BlockSpec((1, l, LANES), lambda i, j: (i, 0, j))
    return pl.pallas_call(
        functools.partial(_lru_kernel, s=s, ch=ROW_TILE),
        grid=(b, ng),
        in_specs=[col, col,
                  pl.BlockSpec((p["lru_cw"].shape[0], LANES), lambda i, j: (0, j)),
                  pl.BlockSpec((1, LANES), lambda i, j: (0, j)),
                  pl.BlockSpec((1, LANES, 4 * LANES), lambda i, j: (j, 0, 0)),
                  pl.BlockSpec((1, 1, 4 * LANES), lambda i, j: (j, 0, 0)),
                  pl.BlockSpec((1, 2, LANES), lambda i, j: (j, 0, 0))],
        out_specs=col,
        out_shape=jax.ShapeDtypeStruct((b, l, d_rnn), BF16),
        scratch_shapes=[pltpu.VMEM((l + 16, LANES), F32), pltpu.VMEM((l, LANES), F32)],
        compiler_params=_params(("parallel", "parallel")),
        name="rglru",
    )(xr, gr, p["lru_cw"], p["lru_cb"], p["lru_wg"], p["lru_bg"], p["lru_lam"])


def _mla_attn_kernel(q_ref, k_ref, v_ref, o_ref, *, s, nlat):
    tq = q_ref.shape[1]
    l = k_ref.shape[1]
    lane = lax.broadcasted_iota(jnp.int32, (tq, LANES), 1)

    def run(kv0, kvn):
        for hp in range(MLA_HEADS // 2):
            vpair = v_ref[0, kv0:kv0 + kvn, hp * LANES:(hp + 1) * LANES]
            outs = []
            for sub in range(2):
                c0 = (2 * hp + sub) * LANES
                sc = _dot_nt(q_ref[0, :, c0:c0 + LANES], k_ref[0, kv0:kv0 + kvn, c0:c0 + LANES])
                e = jnp.exp(sc - jnp.max(sc, axis=-1, keepdims=True))
                den = jnp.sum(e, axis=-1, keepdims=True)
                outs.append(_dot(_bf(e), vpair) / den)
            o_ref[0, :, hp * LANES:(hp + 1) * LANES] = _bf(
                jnp.where(lane < LANES // 2, outs[0], outs[1]))

    t = pl.program_id(1)

    @pl.when(t < nlat)
    def _():
        run(0, l)

    @pl.when(t >= nlat)
    def _():
        run(s, l - s)


def _mla_attn(q, k, v, s, nq_tiles):
    b, l, hl = q.shape
    tq = ROW_TILE
    dv = v.shape[2]
    full = lambda w: pl.BlockSpec((1, l, w), lambda i, t: (i, 0, 0), pipeline_mode=pl.Buffered(1))
    return pl.pallas_call(
        functools.partial(_mla_attn_kernel, s=s, nlat=s // tq),
        grid=(b, nq_tiles),
        in_specs=[pl.BlockSpec((1, tq, hl), lambda i, t: (i, t, 0)), full(hl), full(dv)],
        out_specs=pl.BlockSpec((1, tq, dv), lambda i, t: (i, t, 0)),
        out_shape=jax.ShapeDtypeStruct((b, l, dv), BF16),
        compiler_params=_params(("parallel", "arbitrary")),
        name="mla_attn",
    )(q, k, v)


def _diff_in_kernel(h_ref, mod_ref, g1_ref, win_ref, m_ref, invlen_ref, gainq_ref, gaink_ref,
                    cos_ref, sina_ref, sinb_ref, q_ref, k_ref, v_ref, *, dh):
    tr = h_ref.shape[1]
    d = h_ref.shape[2]
    mod = mod_ref[0, 0]
    u = _rms(h_ref[0], g1_ref[...]) * (1.0 + mod[1:2]) + mod[0:1]
    z = _dot(_bf(u), win_ref[...])
    v_ref[0] = _bf(z[:, 2 * d:])
    cos, sina, sinb = cos_ref[...], sina_ref[...], sinb_ref[...]
    m, invlen = m_ref[...], invlen_ref[...]
    for src, gain_ref, dst in ((z[:, :d], gainq_ref, q_ref), (z[:, d:2 * d], gaink_ref, k_ref)):
        xn = _heads_norm(src, m, invlen, gain_ref[...], DIFF_HEADS)
        for h in range(DIFF_HEADS):
            dst[0, :, h * LANES:(h + 1) * LANES] = _bf(
                _rope(xn[h * tr:(h + 1) * tr], cos, sina, sinb, dh // 4))


def _diff_in(h, modtab, g1, p, rope, nlat):
    b, l, d = h.shape
    tr = ROW_TILE
    row = pl.BlockSpec((1, tr, d), lambda i, t: (i, t, 0))
    tab = pl.BlockSpec((tr, LANES), lambda i, t: (t, 0))
    consts = [g1, p["w_in"], p["m"], p["invlen"], p["gainq"], p["gaink"]]
    return pl.pallas_call(
        functools.partial(_diff_in_kernel, dh=p["dh"]),
        grid=(b, l // tr),
        in_specs=[row, pl.BlockSpec((1, 1, 6, d), lambda i, t: (i, t // nlat, 0, 0))]
                 + [_const_spec(c.shape) for c in consts] + [tab, tab, tab],
        out_specs=[row, row, row],
        out_shape=[jax.ShapeDtypeStruct((b, l, d), BF16)] * 3,
        compiler_params=_params(("parallel", "parallel")),
        name="diff_in",
    )(h, modtab, *consts, *rope)


def _diff_attn_kernel(q_ref, k_ref, v_ref, lamv_ref, gout_ref, o_ref, *, s, nlat, lam_init):
    tq = q_ref.shape[1]
    l = k_ref.shape[1]
    lane = lax.broadcasted_iota(jnp.int32, (tq, LANES), 1)
    lv = lamv_ref[...]
    lam = (jnp.exp(jnp.sum(lv[0:1] * lv[1:2], axis=-1, keepdims=True))
           - jnp.exp(jnp.sum(lv[2:3] * lv[3:4], axis=-1, keepdims=True)) + lam_init)
    gout = gout_ref[...]

    def run(kv0, kvn):
        for h in range(DIFF_HEADS):
            c0 = h * LANES
            qh = q_ref[0, :, c0:c0 + LANES]
            kh = k_ref[0, kv0:kv0 + kvn, c0:c0 + LANES]
            zero = jnp.zeros_like(qh)
            s1 = _dot_nt(jnp.where(lane < LANES // 2, qh, zero), kh)
            s2 = _dot_nt(jnp.where(lane < LANES // 2, zero, qh), kh)
            e1 = jnp.exp(s1 - jnp.max(s1, axis=-1, keepdims=True))
            e2 = jnp.exp(s2 - jnp.max(s2, axis=-1, keepdims=True))
            r1 = 1.0 / jnp.sum(e1, axis=-1, keepdims=True)
            r2 = lam / jnp.sum(e2, axis=-1, keepdims=True)
            o = _dot(_bf(e1 * r1 - e2 * r2), v_ref[0, kv0:kv0 + kvn, c0:c0 + LANES])
            o_ref[0, :, c0:c0 + LANES] = _bf(_rms(o, gout) * (1.0 - lam_init))

    t = pl.program_id(1)

    @pl.when(t < nlat)
    def _():
        run(0, l)

    @pl.when(t >= nlat)
    def _():
        run(s, l - s)


def _diff_attn(q, k, v, lam_vecs, g_out, s, nq_tiles, lam_init):
    b, l, d = q.shape
    tq = ROW_TILE
    full = pl.BlockSpec((1, l, d), lambda i, t: (i, 0, 0), pipeline_mode=pl.Buffered(1))
    tile = pl.BlockSpec((1, tq, d), lambda i, t: (i, t, 0))
    return pl.pallas_call(
        functools.partial(_diff_attn_kernel, s=s, nlat=s // tq, lam_init=lam_init),
        grid=(b, nq_tiles),
        in_specs=[tile, full, full, _const_spec(lam_vecs.shape), _const_spec(g_out.shape)],
        out_specs=tile,
        out_shape=jax.ShapeDtypeStruct((b, l, d), BF16),
        compiler_params=_params(("parallel", "arbitrary")),
        name="diff_attn",
    )(q, k, v, lam_vecs, g_out)


def _post_kernel(hm_ref, hp_ref, hn_ref, y1m_ref, y1p_ref, y1n_ref, y2m_ref, y2p_ref, y2n_ref,
                 mod_ref, g2_ref, wo1_ref, wo2_ref, wup_ref, cw_ref, cb_ref, wdn_ref, o_ref,
                 *, nlat, ntiles, ff_chunk):
    tr = hm_ref.shape[1]
    halo = hp_ref.shape[1]
    ext = tr + 2 * halo
    d_ff = wdn_ref.shape[0]
    mod = mod_ref[0, 0]
    cat = lambda p, m, n: jnp.concatenate([p[0], m[0], n[0]], axis=0)
    y = _dot(cat(y1p_ref, y1m_ref, y1n_ref), wo1_ref[...]) + _dot(cat(y2p_ref, y2m_ref, y2n_ref), wo2_ref[...])
    h1 = cat(hp_ref, hm_ref, hn_ref) + mod[2:3] * y
    u = _bf(_rms(h1, g2_ref[...]) * (1.0 + mod[4:5]) + mod[3:4])

    t = pl.program_id(1)
    row = lax.broadcasted_iota(jnp.int32, (tr, 1), 0)
    seg_start = jnp.logical_or(t == 0, t == nlat)
    seg_end = jnp.logical_or(t == nlat - 1, t == ntiles - 1)
    keep_prev = jnp.logical_not(jnp.logical_and(seg_start, row == 0))
    keep_next = jnp.logical_not(jnp.logical_and(seg_end, row == tr - 1))

    acc = jnp.zeros((tr, hm_ref.shape[2]), F32)
    for c0 in range(0, d_ff, ff_chunk):
        a = _dot(u[halo:halo + tr], wup_ref[:, c0:c0 + ff_chunk])
        g = _dot(u, wup_ref[:, d_ff + c0:d_ff + c0 + ff_chunk])
        cw = cw_ref[:, c0:c0 + ff_chunk]
        g_prev = jnp.where(keep_prev, pltpu.roll(g, 1, 0)[halo:halo + tr], 0.0)
        g_next = jnp.where(keep_next, pltpu.roll(g, ext - 1, 0)[halo:halo + tr], 0.0)
        gc = (cb_ref[:, c0:c0 + ff_chunk] + g_prev * cw[0:1] + g[halo:halo + tr] * cw[1:2]
              + g_next * cw[2:3])
        acc = acc + _dot(_bf(jax.nn.gelu(gc) * a), wdn_ref[c0:c0 + ff_chunk, :])
    o_ref[0] = h1[halo:halo + tr] + mod[5:6] * acc


def _post(h, y1, y2, y2_col, modtab, g2, wo1, wo2, fp, nlat, out_tiles):
    b, l, d = h.shape
    tr, halo = ROW_TILE, BF16_ROWS
    ntiles = l // tr
    per = tr // halo
    last = out_tiles * per - 1
    yw = wo1.shape[0]
    main = lambda w, c: pl.BlockSpec((1, tr, w), lambda i, t: (i, t, c))
    prev = lambda w, c: pl.BlockSpec((1, halo, w), lambda i, t: (i, jnp.maximum(t * per - 1, 0), c))
    nxt = lambda w, c: pl.BlockSpec((1, halo, w), lambda i, t: (i, jnp.minimum((t + 1) * per, last), c))
    consts = [g2, wo1, wo2, fp["w_up"], fp["cw"], fp["cb"], fp["w_down"]]
    d_ff = fp["w_down"].shape[0]
    kern = functools.partial(_post_kernel, nlat=nlat, ntiles=ntiles, ff_chunk=d_ff // 2)
    return pl.pallas_call(
        kern,
        grid=(b, out_tiles),
        in_specs=[main(d, 0), prev(d, 0), nxt(d, 0),
                  main(yw, 0), prev(yw, 0), nxt(yw, 0),
                  main(yw, y2_col), prev(yw, y2_col), nxt(yw, y2_col),
                  pl.BlockSpec((1, 1, 6, d), lambda i, t: (i, t // nlat, 0, 0))]
                 + [_const_spec(c.shape) for c in consts],
        out_specs=main(d, 0),
        out_shape=jax.ShapeDtypeStruct((b, out_tiles * tr, d), F32),
        compiler_params=_params(("parallel", "parallel")),
        name="post",
    )(h, h, h, y1, y1, y1, y2, y2, y2, modtab, *consts)


def _rope_tables(s, l, da, blocks):
    half = da // 2
    rows = s // GRID_W
    row = jnp.repeat(jnp.arange(rows, dtype=F32), GRID_W)
    col = jnp.tile(jnp.arange(GRID_W, dtype=F32), rows)
    inv = ROPE_THETA ** (-jnp.arange(0, da, 2, dtype=F32) / da)
    ar, ac = row[:, None] * inv, col[:, None] * inv
    cr, sr, cc, sc = jnp.cos(ar), jnp.sin(ar), jnp.cos(ac), jnp.sin(ac)
    z = jnp.zeros((s, half), F32)
    cos_p, sina_p, sinb_p = [], [], []
    for blk in blocks:
        if blk[0] == "id":
            cos_p.append(jnp.ones((s, blk[1]), F32))
            sina_p.append(jnp.zeros((s, blk[1]), F32))
            sinb_p.append(jnp.zeros((s, blk[1]), F32))
        else:
            cos_p += [cr, cr, cc, cc]
            sina_p += [-sr, z, -sc, z]
            sinb_p += [z, sr, z, sc]
    pad = lambda x, v: jnp.concatenate([x, jnp.full((l - s, LANES), v, F32)], axis=0)
    return (pad(jnp.concatenate(cos_p, axis=1), 1.0), pad(jnp.concatenate(sina_p, axis=1), 0.0),
            pad(jnp.concatenate(sinb_p, axis=1), 0.0))


def _seg_matrix(seg_ids):
    ids = np.asarray(seg_ids)
    return jnp.asarray((ids[:, None] == ids[None, :]).astype(np.float32), BF16)


def _lanes(*parts):
    return jnp.concatenate([jnp.asarray(p, F32).reshape(-1) for p in parts]).reshape(1, -1)


def _even_params(e, d, w_in_ab, lru_conv_w, lru_conv_b, lru_gate_w, lru_gate_b, lru_lambda, mla_g_q,
                 mla_g_kv, mla_w_uq, mla_w_ukv, gqn, gkn, gqr, gkr, w_out_ab):
    d_rnn = lru_conv_w.shape[2]
    q_rank, kv_rank = mla_g_q.shape[1], mla_g_kv.shape[1]
    nope, rope, hh = MLA_NOPE, MLA_ROPE, MLA_HEADS
    dv = mla_w_ukv.shape[2] // hh - nope
    w_in = w_in_ab[e]
    base = 2 * d_rnn + q_rank + kv_rank
    w_in_pad = jnp.concatenate([w_in[:, :base], jnp.zeros((d, nope), F32), w_in[:, base:],
                                jnp.zeros((d, LANES - nope - rope), F32)], axis=1)
    w_uq = mla_w_uq[e].reshape(q_rank, hh, nope + rope)
    w_uq = jnp.pad(w_uq, ((0, 0), (0, 0), (0, LANES - nope - rope))).reshape(q_rank, hh * LANES)
    w_ukv = mla_w_ukv[e].reshape(kv_rank, hh, nope + dv)
    w_uk = jnp.pad(w_ukv[:, :, :nope], ((0, 0), (0, 0), (0, LANES - nope))).reshape(kv_rank, hh * LANES)
    w_uv = w_ukv[:, :, nope:].reshape(kv_rank, hh * dv)
    scale = (nope + rope) ** -0.5
    rest = LANES - nope - rope
    gw = lru_gate_w[e]
    bs = gw.shape[-1]
    per = LANES // bs
    ng = d_rnn // LANES
    gwg = gw.reshape(2, 2, ng, per, bs, bs)
    eye = jnp.eye(per, dtype=F32)
    wg = jnp.einsum("dgjpcn,pq->jpcdgqn", gwg, eye).reshape(ng, LANES, 4 * LANES)
    bg = lru_gate_b[e].reshape(2, 2, ng, LANES).transpose(2, 0, 1, 3).reshape(ng, 1, 4 * LANES)
    lam = lru_lambda[e].reshape(2, ng, LANES).transpose(1, 0, 2)
    return {
        "d_rnn": d_rnn,
        "w_in": _bf(w_in_pad), "g_q": mla_g_q[e][None], "g_kv": mla_g_kv[e][None],
        "w_uq": _bf(w_uq), "w_uk": _bf(w_uk), "w_uv": _bf(w_uv),
        "m": _seg_matrix([0] * nope + [1] * rope + [2] * rest),
        "invlen": _lanes(np.full(nope, 1.0 / nope), np.full(LANES - nope, 1.0 / rope)),
        "gainq": _lanes(gqn[e] * scale, gqr[e] * scale, np.zeros(rest)),
        "gaink": _lanes(gkn[e], np.zeros(LANES - nope)),
        "gkr": _lanes(np.zeros(nope), gkr[e], np.zeros(rest)),
        "lru_cw": lru_conv_w[e], "lru_cb": lru_conv_b[e][None],
        "lru_wg": _bf(wg), "lru_bg": bg, "lru_lam": lam,
        "wo1": _bf(w_out_ab[e][:d_rnn]), "wo2": _bf(w_out_ab[e][d_rnn:]),
    }


def _odd_params(o, d, w_in_c, diff_gq, diff_gk, diff_lam, diff_g_out, w_out_c):
    dh = diff_gq.shape[1]
    scale = dh ** -0.5
    return {
        "dh": dh,
        "w_in": _bf(w_in_c[o]),
        "m": _seg_matrix([0] * dh + [1] * dh),
        "invlen": _lanes(np.full(LANES, 1.0 / dh)),
        "gainq": _lanes(diff_gq[o] * scale, diff_gq[o] * scale),
        "gaink": _lanes(diff_gk[o], diff_gk[o]),
        "lam_vecs": diff_lam[o], "g_out": diff_g_out[o][None],
        "wo1": _bf(w_out_c[o][:d // 2]), "wo2": _bf(w_out_c[o][d // 2:]),
    }


def kernel(x, c, ctx, c_ctx, w_mod, b_mod, g_norm1, g_norm2, w_in_ab, lru_conv_w, lru_conv_b, lru_gate_w, lru_gate_b, lru_lambda, mla_g_q, mla_g_kv, mla_w_uq, mla_w_ukv, mla_gq_nope, mla_gk_nope, mla_gq_rope, mla_gk_rope, w_out_ab, w_in_c, diff_gq, diff_gk, diff_lam, diff_g_out, w_out_c, ffn_w_up, ffn_conv_w, ffn_conv_b, ffn_w_down):
    b, s, d = x.shape
    n_ctx = ctx.shape[1]
    l = s + n_ctx
    depth = w_mod.shape[0]
    tr = ROW_TILE
    assert s % tr == 0 and n_ctx == tr and s % GRID_W == 0 and d == DIFF_HEADS * LANES
    nlat, ntiles = s // tr, l // tr

    cc = jnp.concatenate([c, c_ctx[None], jnp.zeros((BF16_ROWS - (b + 1) % BF16_ROWS, d), F32)], axis=0)
    mod = _modulation(cc, w_mod, b_mod)
    mod_lat = mod[:, :b].reshape(depth, b, 1, 6, d)
    mod_ctx = jnp.broadcast_to(mod[:, b].reshape(depth, 1, 1, 6, d), (depth, b, 1, 6, d))
    modtab = jnp.concatenate([mod_lat, mod_ctx], axis=2)

    rope_mla = _rope_tables(s, l, MLA_ROPE // 2,
                            [("id", MLA_NOPE), ("rot",), ("id", LANES - MLA_NOPE - MLA_ROPE)])
    dh = diff_gq.shape[1]
    rope_diff = _rope_tables(s, l, dh // 2, [("rot",), ("rot",)])

    h = jnp.concatenate([x, ctx], axis=1)
    for li in range(depth):
        last = li == depth - 1
        out_tiles = nlat if last else ntiles
        fp = {"w_up": _bf(ffn_w_up[li]), "cw": ffn_conv_w[li], "cb": ffn_conv_b[li][None],
              "w_down": _bf(ffn_w_down[li])}
        g1, g2 = g_norm1[li][None], g_norm2[li][None]
        if li % 2 == 0:
            p = _even_params(li // 2, d, w_in_ab, lru_conv_w, lru_conv_b, lru_gate_w, lru_gate_b,
                             lru_lambda, mla_g_q, mla_g_kv, mla_w_uq, mla_w_ukv, mla_gq_nope,
                             mla_gk_nope, mla_gq_rope, mla_gk_rope, w_out_ab)
            xr, gr, q, k, v = _even_in(h, modtab[li], g1, p, rope_mla, nlat)
            y1 = _lru(xr, gr, p, s)
            y2 = _mla_attn(q, k, v, s, out_tiles)
            h = _post(h, y1, y2, 0, modtab[li], g2, p["wo1"], p["wo2"], fp, nlat, out_tiles)
        else:
            p = _odd_params(li // 2, d, w_in_c, diff_gq, diff_gk, diff_lam, diff_g_out, w_out_c)
            lam_init = 0.8 - 0.6 * math.exp(-0.3 * li)
            q, k, v = _diff_in(h, modtab[li], g1, p, rope_diff, nlat)
            y = _diff_attn(q, k, v, p["lam_vecs"], p["g_out"], s, out_tiles, lam_init)
            h = _post(h, y, y, 1, modtab[li], g2, p["wo1"], p["wo2"], fp, nlat, out_tiles)
    return h
```

```python
import functools
import math

import numpy as np
import jax
import jax.numpy as jnp
from jax import lax
from jax.experimental import pallas as pl
from jax.experimental.pallas import tpu as pltpu

F32 = jnp.float32
BF16 = jnp.bfloat16

EPS = 1e-6
ROPE_THETA = 10000.0
GRID_W = 64
RG_C = 8.0
LRU_BLOCKS = 8
MLA_HEADS = 8
MLA_NOPE = 64
MLA_ROPE = 32
DIFF_HEADS = 8

LANES = 128
BF16_ROWS = 16
ROW_TILE = 256
ATTN_UNROLL = 8
LOG2E = 1.4426950408889634
VMEM_LIMIT = 52 * 1024 * 1024


def _bf(x):
    return x.astype(BF16)


def _dot(a, b):
    return jnp.dot(a, b, preferred_element_type=F32)


def _dot_nt(a, b):
    return lax.dot_general(a, b, (((1,), (1,)), ((), ())), preferred_element_type=F32)


def _rms(x, g):
    ms = jnp.mean(x * x, axis=-1, keepdims=True)
    return x * lax.rsqrt(ms + EPS) * g


def _expm1(x):
    u = jnp.exp(x)
    small = jnp.where(u == 1.0, x, (u - 1.0) * x / jnp.log(u))
    return jnp.where(jnp.abs(x) > 1.0, u - 1.0, small)


def _log1p(y):
    w = 1.0 + y
    return jnp.where(w == 1.0, y, jnp.log(w) * y / (w - 1.0))


def _seg_sumsq(x, m):
    x2 = x * x
    hi = _bf(x2)
    lo = _bf(x2 - hi.astype(F32))
    return _dot(hi, m) + _dot(lo, m)


def _heads_norm(x, m, invlen, gain, nheads):
    stacked = jnp.concatenate([x[:, h * LANES:(h + 1) * LANES] for h in range(nheads)], axis=0)
    ss = _seg_sumsq(stacked, m)
    return stacked * lax.rsqrt(ss * invlen + EPS) * gain


def _rope(x, cos, sina, sinb, shift):
    return x * cos + pltpu.roll(x, LANES - shift, 1) * sina + pltpu.roll(x, shift, 1) * sinb


def _const_spec(shape):
    n = len(shape)
    return pl.BlockSpec(shape, lambda *_: (0,) * n, pipeline_mode=pl.Buffered(1))


def _params(sem):
    return pltpu.CompilerParams(dimension_semantics=sem, vmem_limit_bytes=VMEM_LIMIT)


def _mod_kernel(cc_ref, w_ref, b_ref, o_ref):
    x = cc_ref[...]
    sx = x * jax.nn.sigmoid(x)
    o_ref[0] = _dot(_bf(sx), _bf(w_ref[0])) + b_ref[0]


def _modulation(cc, w_mod, b_mod):
    depth, d, d6 = w_mod.shape
    rows = cc.shape[0]
    return pl.pallas_call(
        _mod_kernel,
        grid=(depth, d6 // d),
        in_specs=[pl.BlockSpec((rows, d), lambda l, j: (0, 0)),
                  pl.BlockSpec((1, d, d), lambda l, j: (l, 0, j)),
                  pl.BlockSpec((1, 1, d), lambda l, j: (l, 0, j))],
        out_specs=pl.BlockSpec((1, rows, d), lambda l, j: (l, 0, j)),
        out_shape=jax.ShapeDtypeStruct((depth, rows, d6), F32),
        compiler_params=_params(("arbitrary", "arbitrary")),
        name="modulation",
    )(cc, w_mod, b_mod.reshape(depth, 1, d6))


def _even_in_kernel(h_ref, mod_ref, g1_ref, win_ref, gq_ref, gkv_ref, wuq_ref, wuk_ref, wuv_ref,
                    m_ref, invlen_ref, gainq_ref, gaink_ref, gkr_ref, cos_ref, sina_ref, sinb_ref,
                    xr_ref, gr_ref, q_ref, k_ref, vt_ref, *, d_rnn, q_rank, kv_rank):
    tr = h_ref.shape[1]
    mod = mod_ref[0, 0]
    u = _rms(h_ref[0], g1_ref[...]) * (1.0 + mod[1:2]) + mod[0:1]
    z = _dot(_bf(u), win_ref[...])
    o = 2 * d_rnn
    xr_ref[0] = z[:, :d_rnn]
    gr_ref[0] = z[:, d_rnn:o]
    ql = z[:, o:o + q_rank]
    kvl = z[:, o + q_rank:o + q_rank + kv_rank]
    kr = z[:, o + q_rank + kv_rank:]

    cos, sina, sinb = cos_ref[...], sina_ref[...], sinb_ref[...]
    m, invlen = m_ref[...], invlen_ref[...]

    q = _dot(_bf(_rms(ql, gq_ref[...])), wuq_ref[...])
    qn = _heads_norm(q, m, invlen, gainq_ref[...], MLA_HEADS)
    for h in range(MLA_HEADS):
        q_ref[0, :, h * LANES:(h + 1) * LANES] = _bf(
            _rope(qn[h * tr:(h + 1) * tr], cos, sina, sinb, MLA_ROPE // 4))

    kvn = _bf(_rms(kvl, gkv_ref[...]))
    vt_ref[0, 0] = _bf(_dot(kvn, wuv_ref[...]).T)
    kn = _heads_norm(_dot(kvn, wuk_ref[...]), m, invlen, gaink_ref[...], MLA_HEADS)
    ss = jnp.sum(kr * kr, axis=-1, keepdims=True) * (1.0 / MLA_ROPE)
    k_rope = _rope(kr * lax.rsqrt(ss + EPS) * gkr_ref[...], cos, sina, sinb, MLA_ROPE // 4)
    for h in range(MLA_HEADS):
        k_ref[0, :, h * LANES:(h + 1) * LANES] = _bf(kn[h * tr:(h + 1) * tr] + k_rope)


def _even_in(h, modtab, g1, p, rope, nlat):
    b, l, d = h.shape
    tr = ROW_TILE
    d_rnn = p["d_rnn"]
    hl = MLA_HEADS * LANES
    hdv = p["w_uv"].shape[1]
    row = lambda w: pl.BlockSpec((1, tr, w), lambda i, t: (i, t, 0))
    tab = pl.BlockSpec((tr, LANES), lambda i, t: (t, 0))
    consts = [g1, p["w_in"], p["g_q"], p["g_kv"], p["w_uq"], p["w_uk"], p["w_uv"],
              p["m"], p["invlen"], p["gainq"], p["gaink"], p["gkr"]]
    kern = functools.partial(_even_in_kernel, d_rnn=d_rnn, q_rank=p["g_q"].shape[1],
                             kv_rank=p["g_kv"].shape[1])
    return pl.pallas_call(
        kern,
        grid=(b, l // tr),
        in_specs=[row(d), pl.BlockSpec((1, 1, 6, d), lambda i, t: (i, t // nlat, 0, 0))]
                 + [_const_spec(c.shape) for c in consts] + [tab, tab, tab],
        out_specs=[row(d_rnn), row(d_rnn), row(hl), row(hl),
                   pl.BlockSpec((1, 1, hdv, tr), lambda i, t: (i, t, 0, 0))],
        out_shape=[jax.ShapeDtypeStruct((b, l, d_rnn), F32), jax.ShapeDtypeStruct((b, l, d_rnn), F32),
                   jax.ShapeDtypeStruct((b, l, hl), BF16), jax.ShapeDtypeStruct((b, l, hl), BF16),
                   jax.ShapeDtypeStruct((b, l // tr, hdv, tr), BF16)],
        compiler_params=_params(("parallel", "parallel")),
        name="even_in",
    )(h, modtab, *consts, *rope)


def _scan_chunk(a, b, reverse, row):
    ch = a.shape[0]
    d = 1
    while d < ch:
        if not reverse:
            if d < 8:
                keep = row >= d
                a_sh = jnp.where(keep, pltpu.roll(a, d, 0), 1.0)
                b_sh = jnp.where(keep, pltpu.roll(b, d, 0), 0.0)
            else:
                a_sh = jnp.concatenate([jnp.ones((d, LANES), F32), a[:ch - d]], axis=0)
                b_sh = jnp.concatenate([jnp.zeros((d, LANES), F32), b[:ch - d]], axis=0)
        else:
            if d < 8:
                keep = row < ch - d
                a_sh = jnp.where(keep, pltpu.roll(a, ch - d, 0), 1.0)
                b_sh = jnp.where(keep, pltpu.roll(b, ch - d, 0), 0.0)
            else:
                a_sh = jnp.concatenate([a[d:], jnp.ones((d, LANES), F32)], axis=0)
                b_sh = jnp.concatenate([b[d:], jnp.zeros((d, LANES), F32)], axis=0)
        b = a * b_sh + b
        a = a * a_sh
        d *= 2
    return a, b


def _lru_kernel(xr_ref, gr_ref, cw_ref, cb_ref, wg_ref, bg_ref, lam_ref, y_ref, xpad, hf, *, s, ch):
    l = xr_ref.shape[1]
    nlat, nctx = s // ch, (l - s) // ch
    conv_w = cw_ref.shape[0]
    ext_rows = ch + 16
    xpad[0:8, :] = jnp.zeros((8, LANES), F32)
    xpad[8:8 + l, :] = xr_ref[0]
    xpad[8 + l:16 + l, :] = jnp.zeros((8, LANES), F32)
    cw, cb = cw_ref[...], cb_ref[...]
    nl = -lam_ref[0]
    sp = jnp.maximum(nl, 0.0) + _log1p(jnp.exp(-jnp.abs(nl)))
    row = lax.broadcasted_iota(jnp.int32, (ch, LANES), 0)

    def chunk(r0, direction, carry, in_ctx):
        ext = xpad[pl.ds(r0, ext_rows), :]
        x = jnp.zeros((ch, LANES), F32) + cb
        for w in range(conv_w):
            xs = pltpu.roll(ext, ext_rows - (7 + w), 0)[0:ch]
            src = r0 + row + (w - 1)
            if in_ctx and w == 0:
                xs = jnp.where(src >= s, xs, 0.0)
            if not in_ctx and w >= 2:
                xs = jnp.where(src < s, xs, 0.0)
            x = x + xs * cw[w:w + 1]
        c0 = direction * 2 * LANES
        pre = _dot(_bf(x), wg_ref[0, :, c0:c0 + 2 * LANES]) + bg_ref[0, :, c0:c0 + 2 * LANES]
        r = jax.nn.sigmoid(pre[:, :LANES])
        i = jax.nn.sigmoid(pre[:, LANES:])
        log_a = -RG_C * r * sp[direction:direction + 1]
        a = jnp.exp(log_a)
        bb = jnp.sqrt(-_expm1(2.0 * log_a)) * (i * x)
        a_c, b_c = _scan_chunk(a, bb, direction == 1, row)
        hh = b_c + a_c * carry
        carry = hh[0:1] if direction == 1 else hh[ch - 1:ch]
        return hh, carry

    carry = jnp.zeros((1, LANES), F32)
    for c in range(nctx):
        r0 = s + c * ch
        hh, carry = chunk(r0, 0, carry, True)
        hf[r0:r0 + ch, :] = hh

    def fwd(i, carry):
        r0 = pl.multiple_of(i * ch, ch)
        hh, carry = chunk(r0, 0, carry, False)
        hf[pl.ds(r0, ch), :] = hh
        return carry

    lax.fori_loop(0, nlat, fwd, carry)

    def emit(r0, hb):
        rows = pl.ds(r0, ch)
        y_ref[0, rows, :] = _bf((hf[rows, :] + hb) * jax.nn.gelu(gr_ref[0, rows, :]))

    carry = jnp.zeros((1, LANES), F32)
    for c in reversed(range(nctx)):
        r0 = s + c * ch
        hb, carry = chunk(r0, 1, carry, True)
        emit(r0, hb)

    def bwd(i, carry):
        r0 = pl.multiple_of((nlat - 1 - i) * ch, ch)
        hb, carry = chunk(r0, 1, carry, False)
        emit(r0, hb)
        return carry

    lax.fori_loop(0, nlat, bwd, carry)


def _lru(xr, gr, p, s):
    b, l, d_rnn = xr.shape
    ng = d_rnn // LANES
    col = pl.BlockSpec((1, l, LANES), lambda i, j: (i, 0, j))
    return pl.pallas_call(
        functools.partial(_lru_kernel, s=s, ch=ROW_TILE),
        grid=(b, ng),
        in_specs=[col, col,
                  pl.BlockSpec((p["lru_cw"].shape[0], LANES), lambda i, j: (0, j)),
                  pl.BlockSpec((1, LANES), lambda i, j: (0, j)),
                  pl.BlockSpec((1, LANES, 4 * LANES), lambda i, j: (j, 0, 0)),
                  pl.BlockSpec((1, 1, 4 * LANES), lambda i, j: (j, 0, 0)),
                  pl.BlockSpec((1, 2, LANES), lambda i, j: (j, 0, 0))],
        out_specs=col,
        out_shape=jax.ShapeDtypeStruct((b, l, d_rnn), BF16),
        scratch_shapes=[pltpu.VMEM((l + 16, LANES), F32), pltpu.VMEM((l, LANES), F32)],
        compiler_params=_params(("parallel", "parallel")),
        name="rglru",
    )(xr, gr, p["lru_cw"], p["lru_cb"], p["lru_wg"], p["lru_bg"], p["lru_lam"])


def _attn_kernel(*refs, diff, s, lam_init):
    if diff:
        q_ref, k_ref, vt_ref, lamv_ref, gout_ref, o_ref, s_buf, e_buf = refs
    else:
        q_ref, k_ref, vt_ref, o_ref, s_buf, e_buf = refs
    tq = q_ref.shape[1]
    nchunks, hdv, kc = vt_ref.shape[1:]
    nheads = q_ref.shape[2] // LANES
    dv = hdv // nheads
    per_block = LANES // dv
    nsub = 2 if diff else 1
    grp = kc // 8
    subs = range(nsub)
    if diff:
        lane = lax.broadcasted_iota(jnp.int32, (tq, LANES), 1)
        lv = lamv_ref[...]
        lam = (jnp.exp(jnp.sum(lv[0:1] * lv[1:2], axis=-1, keepdims=True))
               - jnp.exp(jnp.sum(lv[2:3] * lv[3:4], axis=-1, keepdims=True)) + lam_init)
        gout = gout_ref[...]

    def q_operands(h):
        qh = q_ref[0, :, h * LANES:(h + 1) * LANES]
        if not diff:
            return [qh]
        zq = jnp.zeros_like(qh)
        first = lane < LANES // 2
        return [jnp.where(first, qh, zq), jnp.where(first, zq, qh)]

    def fold(x, op):
        return op(x.reshape(grp, 8, tq), axis=0)

    def run(j_lo, j_hi):
        mrun_prev = lsum_prev = None
        pend = []
        for t in range(nheads + 2):
            ha, hb, hc = t, t - 1, t - 2
            do_a, do_b, do_c = ha < nheads, 0 <= hb < nheads, 0 <= hc < nheads
            if do_a:
                qs = q_operands(ha)
            if do_c:
                ls = [jnp.sum(lsum_prev[i], axis=0, keepdims=True) for i in subs]
                rinv = 1.0 / ls[0]
                if diff:
                    cb = jnp.broadcast_to(_bf(lam * ls[0] / ls[1]), (kc, tq))
            if do_b:
                m8 = [jnp.broadcast_to(jnp.max(mrun_prev[i], axis=0, keepdims=True), (8, tq))
                      for i in subs]

            def chunk(j, carry):
                mrun, lsum, oacc = carry
                if do_b:
                    sb = [s_buf[i, j] for i in subs]
                if do_c:
                    eb = [e_buf[i, j] for i in subs]
                if do_a:
                    r0 = j * kc if isinstance(j, int) else pl.multiple_of(j * kc, kc)
                    kj = k_ref[0, pl.ds(r0, kc), ha * LANES:(ha + 1) * LANES]
                    sc = [_dot_nt(kj, qs[i]) for i in subs]
                    for i in subs:
                        s_buf[i, j] = sc[i]
                    mrun = tuple(jnp.maximum(mrun[i], fold(sc[i], jnp.max)) for i in subs)
                if do_b:
                    e = [jnp.exp2(sb[i].reshape(grp, 8, tq) - m8[i]) for i in subs]
                    lsum = tuple(lsum[i] + jnp.sum(e[i], axis=0) for i in subs)
                    for i in subs:
                        e_buf[i, j] = _bf(e[i].reshape(kc, tq))
                if do_c:
                    w = eb[0] - eb[1] * cb if diff else eb[0]
                    oacc = oacc + _dot(vt_ref[0, j, hc * dv:(hc + 1) * dv, :], w)
                return mrun, lsum, oacc

            carry = (tuple(jnp.full((8, tq), -jnp.inf, F32) for _ in subs),
                     tuple(jnp.zeros((8, tq), F32) for _ in subs),
                     jnp.zeros((dv, tq), F32))
            nfull = (j_hi - j_lo) // ATTN_UNROLL

            def body(i, carry):
                for u in range(ATTN_UNROLL):
                    carry = chunk(j_lo + i * ATTN_UNROLL + u, carry)
                return carry

            if nfull:
                carry = lax.fori_loop(0, nfull, body, carry)
            for j in range(j_lo + nfull * ATTN_UNROLL, j_hi):
                carry = chunk(j, carry)
            mrun_prev, lsum_prev, oacc = carry
            if do_c:
                pend.append(oacc * rinv)
                if len(pend) == per_block:
                    o = (pend[0] if per_block == 1 else jnp.concatenate(pend, axis=0)).T
                    if diff:
                        o = _rms(o, gout) * (1.0 - lam_init)
                    c0 = (hc // per_block) * LANES
                    o_ref[0, :, c0:c0 + LANES] = _bf(o)
                    pend = []

    t = pl.program_id(1)
    nlat = s // tq

    @pl.when(t < nlat)
    def _():
        run(0, nchunks)

    @pl.when(t >= nlat)
    def _():
        run(s // kc, nchunks)


def _attention(q, k, vt, s, nq_tiles, extra=(), lam_init=None):
    b, l, hl = q.shape
    tq = ROW_TILE
    nchunks, hdv, kc = vt.shape[1:]
    diff = lam_init is not None
    nsub = 2 if diff else 1
    whole = lambda a: pl.BlockSpec((1,) + a.shape[1:], lambda i, t: (i,) + (0,) * (a.ndim - 1),
                                   pipeline_mode=pl.Buffered(1))
    return pl.pallas_call(
        functools.partial(_attn_kernel, diff=diff, s=s, lam_init=lam_init),
        grid=(b, nq_tiles),
        in_specs=[pl.BlockSpec((1, tq, hl), lambda i, t: (i, t, 0)), whole(k), whole(vt)]
                 + [_const_spec(x.shape) for x in extra],
        out_specs=pl.BlockSpec((1, tq, hdv), lambda i, t: (i, t, 0)),
        out_shape=jax.ShapeDtypeStruct((b, l, hdv), BF16),
        scratch_shapes=[pltpu.VMEM((nsub, nchunks, kc, tq), F32), pltpu.VMEM((nsub, nchunks, kc, tq), BF16)],
        compiler_params=_params(("parallel", "arbitrary")),
        name="diff_attn" if diff else "mla_attn",
    )(q, k, vt, *extra)


def _diff_in_kernel(h_ref, mod_ref, g1_ref, win_ref, m_ref, invlen_ref, gainq_ref, gaink_ref,
                    cos_ref, sina_ref, sinb_ref, q_ref, k_ref, vt_ref, *, dh):
    tr = h_ref.shape[1]
    d = h_ref.shape[2]
    mod = mod_ref[0, 0]
    u = _rms(h_ref[0], g1_ref[...]) * (1.0 + mod[1:2]) + mod[0:1]
    z = _dot(_bf(u), win_ref[...])
    vt_ref[0, 0] = _bf(z[:, 2 * d:].T)
    cos, sina, sinb = cos_ref[...], sina_ref[...], sinb_ref[...]
    m, invlen = m_ref[...], invlen_ref[...]
    for src, gain_ref, dst in ((z[:, :d], gainq_ref, q_ref), (z[:, d:2 * d], gaink_ref, k_ref)):
        xn = _heads_norm(src, m, invlen, gain_ref[...], DIFF_HEADS)
        for h in range(DIFF_HEADS):
            dst[0, :, h * LANES:(h + 1) * LANES] = _bf(
                _rope(xn[h * tr:(h + 1) * tr], cos, sina, sinb, dh // 4))


def _diff_in(h, modtab, g1, p, rope, nlat):
    b, l, d = h.shape
    tr = ROW_TILE
    row = pl.BlockSpec((1, tr, d), lambda i, t: (i, t, 0))
    tab = pl.BlockSpec((tr, LANES), lambda i, t: (t, 0))
    consts = [g1, p["w_in"], p["m"], p["invlen"], p["gainq"], p["gaink"]]
    return pl.pallas_call(
        functools.partial(_diff_in_kernel, dh=p["dh"]),
        grid=(b, l // tr),
        in_specs=[row, pl.BlockSpec((1, 1, 6, d), lambda i, t: (i, t // nlat, 0, 0))]
                 + [_const_spec(c.shape) for c in consts] + [tab, tab, tab],
        out_specs=[row, row, pl.BlockSpec((1, 1, d, tr), lambda i, t: (i, t, 0, 0))],
        out_shape=[jax.ShapeDtypeStruct((b, l, d), BF16), jax.ShapeDtypeStruct((b, l, d), BF16),
                   jax.ShapeDtypeStruct((b, l // tr, d, tr), BF16)],
        compiler_params=_params(("parallel", "parallel")),
        name="diff_in",
    )(h, modtab, *consts, *rope)


def _post_kernel(hm_ref, hp_ref, hn_ref, y1m_ref, y1p_ref, y1n_ref, y2m_ref, y2p_ref, y2n_ref,
                 mod_ref, g2_ref, wo1_ref, wo2_ref, wup_ref, cw_ref, cb_ref, wdn_ref, o_ref,
                 *, nlat, ntiles, ff_chunk):
    tr = hm_ref.shape[1]
    halo = hp_ref.shape[1]
    ext = tr + 2 * halo
    d_ff = wdn_ref.shape[0]
    mod = mod_ref[0, 0]
    cat = lambda p, m, n: jnp.concatenate([p[0], m[0], n[0]], axis=0)
    y = _dot(cat(y1p_ref, y1m_ref, y1n_ref), wo1_ref[...]) + _dot(cat(y2p_ref, y2m_ref, y2n_ref), wo2_ref[...])
    h1 = cat(hp_ref, hm_ref, hn_ref) + mod[2:3] * y
    u = _bf(_rms(h1, g2_ref[...]) * (1.0 + mod[4:5]) + mod[3:4])

    t = pl.program_id(1)
    row = lax.broadcasted_iota(jnp.int32, (tr, 1), 0)
    seg_start = jnp.logical_or(t == 0, t == nlat)
    seg_end = jnp.logical_or(t == nlat - 1, t == ntiles - 1)
    keep_prev = jnp.logical_not(jnp.logical_and(seg_start, row == 0))
    keep_next = jnp.logical_not(jnp.logical_and(seg_end, row == tr - 1))

    acc = jnp.zeros((tr, hm_ref.shape[2]), F32)
    for c0 in range(0, d_ff, ff_chunk):
        a = _dot(u[halo:halo + tr], wup_ref[:, c0:c0 + ff_chunk])
        g = _dot(u, wup_ref[:, d_ff + c0:d_ff + c0 + ff_chunk])
        cw = cw_ref[:, c0:c0 + ff_chunk]
        g_prev = jnp.where(keep_prev, pltpu.roll(g, 1, 0)[halo:halo + tr], 0.0)
        g_next = jnp.where(keep_next, pltpu.roll(g, ext - 1, 0)[halo:halo + tr], 0.0)
        gc = (cb_ref[:, c0:c0 + ff_chunk] + g_prev * cw[0:1] + g[halo:halo + tr] * cw[1:2]
              + g_next * cw[2:3])
        acc = acc + _dot(_bf(jax.nn.gelu(gc) * a), wdn_ref[c0:c0 + ff_chunk, :])
    o_ref[0] = h1[halo:halo + tr] + mod[5:6] * acc


def _post(h, y1, y2, y2_col, modtab, g2, wo1, wo2, fp, nlat, out_tiles):
    b, l, d = h.shape
    tr, halo = ROW_TILE, BF16_ROWS
    ntiles = l // tr
    per = tr // halo
    last = out_tiles * per - 1
    yw = wo1.shape[0]
    main = lambda w, c: pl.BlockSpec((1, tr, w), lambda i, t: (i, t, c))
    prev = lambda w, c: pl.BlockSpec((1, halo, w), lambda i, t: (i, jnp.maximum(t * per - 1, 0), c))
    nxt = lambda w, c: pl.BlockSpec((1, halo, w), lambda i, t: (i, jnp.minimum((t + 1) * per, last), c))
    consts = [g2, wo1, wo2, fp["w_up"], fp["cw"], fp["cb"], fp["w_down"]]
    d_ff = fp["w_down"].shape[0]
    kern = functools.partial(_post_kernel, nlat=nlat, ntiles=ntiles, ff_chunk=d_ff // 2)
    return pl.pallas_call(
        kern,
        grid=(b, out_tiles),
        in_specs=[main(d, 0), prev(d, 0), nxt(d, 0),
                  main(yw, 0), prev(yw, 0), nxt(yw, 0),
                  main(yw, y2_col), prev(yw, y2_col), nxt(yw, y2_col),
                  pl.BlockSpec((1, 1, 6, d), lambda i, t: (i, t // nlat, 0, 0))]
                 + [_const_spec(c.shape) for c in consts],
        out_specs=main(d, 0),
        out_shape=jax.ShapeDtypeStruct((b, out_tiles * tr, d), F32),
        compiler_params=_params(("parallel", "parallel")),
        name="post",
    )(h, h, h, y1, y1, y1, y2, y2, y2, modtab, *consts)


def _rope_tables(s, l, da, blocks):
    half = da // 2
    rows = s // GRID_W
    row = jnp.repeat(jnp.arange(rows, dtype=F32), GRID_W)
    col = jnp.tile(jnp.arange(GRID_W, dtype=F32), rows)
    inv = ROPE_THETA ** (-jnp.arange(0, da, 2, dtype=F32) / da)
    ar, ac = row[:, None] * inv, col[:, None] * inv
    cr, sr, cc, sc = jnp.cos(ar), jnp.sin(ar), jnp.cos(ac), jnp.sin(ac)
    z = jnp.zeros((s, half), F32)
    cos_p, sina_p, sinb_p = [], [], []
    for blk in blocks:
        if blk[0] == "id":
            cos_p.append(jnp.ones((s, blk[1]), F32))
            sina_p.append(jnp.zeros((s, blk[1]), F32))
            sinb_p.append(jnp.zeros((s, blk[1]), F32))
        else:
            cos_p += [cr, cr, cc, cc]
            sina_p += [-sr, z, -sc, z]
            sinb_p += [z, sr, z, sc]
    pad = lambda x, v: jnp.concatenate([x, jnp.full((l - s, LANES), v, F32)], axis=0)
    return (pad(jnp.concatenate(cos_p, axis=1), 1.0), pad(jnp.concatenate(sina_p, axis=1), 0.0),
            pad(jnp.concatenate(sinb_p, axis=1), 0.0))


def _seg_matrix(seg_ids):
    ids = np.asarray(seg_ids)
    return jnp.asarray((ids[:, None] == ids[None, :]).astype(np.float32), BF16)


def _lanes(*parts):
    return jnp.concatenate([jnp.asarray(p, F32).reshape(-1) for p in parts]).reshape(1, -1)


def _even_params(e, d, w_in_ab, lru_conv_w, lru_conv_b, lru_gate_w, lru_gate_b, lru_lambda, mla_g_q,
                 mla_g_kv, mla_w_uq, mla_w_ukv, gqn, gkn, gqr, gkr, w_out_ab):
    d_rnn = lru_conv_w.shape[2]
    q_rank, kv_rank = mla_g_q.shape[1], mla_g_kv.shape[1]
    nope, rope, hh = MLA_NOPE, MLA_ROPE, MLA_HEADS
    dv = mla_w_ukv.shape[2] // hh - nope
    w_in = w_in_ab[e]
    base = 2 * d_rnn + q_rank + kv_rank
    w_in_pad = jnp.concatenate([w_in[:, :base], jnp.zeros((d, nope), F32), w_in[:, base:],
                                jnp.zeros((d, LANES - nope - rope), F32)], axis=1)
    w_uq = mla_w_uq[e].reshape(q_rank, hh, nope + rope)
    w_uq = jnp.pad(w_uq, ((0, 0), (0, 0), (0, LANES - nope - rope))).reshape(q_rank, hh * LANES)
    w_ukv = mla_w_ukv[e].reshape(kv_rank, hh, nope + dv)
    w_uk = jnp.pad(w_ukv[:, :, :nope], ((0, 0), (0, 0), (0, LANES - nope))).reshape(kv_rank, hh * LANES)
    w_uv = w_ukv[:, :, nope:].reshape(kv_rank, hh * dv)
    scale = (nope + rope) ** -0.5 * LOG2E
    rest = LANES - nope - rope
    gw = lru_gate_w[e]
    bs = gw.shape[-1]
    per = LANES // bs
    ng = d_rnn // LANES
    gwg = gw.reshape(2, 2, ng, per, bs, bs)
    eye = jnp.eye(per, dtype=F32)
    wg = jnp.einsum("dgjpcn,pq->jpcdgqn", gwg, eye).reshape(ng, LANES, 4 * LANES)
    bg = lru_gate_b[e].reshape(2, 2, ng, LANES).transpose(2, 0, 1, 3).reshape(ng, 1, 4 * LANES)
    lam = lru_lambda[e].reshape(2, ng, LANES).transpose(1, 0, 2)
    return {
        "d_rnn": d_rnn,
        "w_in": _bf(w_in_pad), "g_q": mla_g_q[e][None], "g_kv": mla_g_kv[e][None],
        "w_uq": _bf(w_uq), "w_uk": _bf(w_uk), "w_uv": _bf(w_uv),
        "m": _seg_matrix([0] * nope + [1] * rope + [2] * rest),
        "invlen": _lanes(np.full(nope, 1.0 / nope), np.full(LANES - nope, 1.0 / rope)),
        "gainq": _lanes(gqn[e] * scale, gqr[e] * scale, np.zeros(rest)),
        "gaink": _lanes(gkn[e], np.zeros(LANES - nope)),
        "gkr": _lanes(np.zeros(nope), gkr[e], np.zeros(rest)),
        "lru_cw": lru_conv_w[e], "lru_cb": lru_conv_b[e][None],
        "lru_wg": _bf(wg), "lru_bg": bg, "lru_lam": lam,
        "wo1": _bf(w_out_ab[e][:d_rnn]), "wo2": _bf(w_out_ab[e][d_rnn:]),
    }


def _odd_params(o, d, w_in_c, diff_gq, diff_gk, diff_lam, diff_g_out, w_out_c):
    dh = diff_gq.shape[1]
    scale = dh ** -0.5 * LOG2E
    return {
        "dh": dh,
        "w_in": _bf(w_in_c[o]),
        "m": _seg_matrix([0] * dh + [1] * dh),
        "invlen": _lanes(np.full(LANES, 1.0 / dh)),
        "gainq": _lanes(diff_gq[o] * scale, diff_gq[o] * scale),
        "gaink": _lanes(diff_gk[o], diff_gk[o]),
        "lam_vecs": diff_lam[o], "g_out": diff_g_out[o][None],
        "wo1": _bf(w_out_c[o][:d // 2]), "wo2": _bf(w_out_c[o][d // 2:]),
    }


def kernel(x, c, ctx, c_ctx, w_mod, b_mod, g_norm1, g_norm2, w_in_ab, lru_conv_w, lru_conv_b, lru_gate_w, lru_gate_b, lru_lambda, mla_g_q, mla_g_kv, mla_w_uq, mla_w_ukv, mla_gq_nope, mla_gk_nope, mla_gq_rope, mla_gk_rope, w_out_ab, w_in_c, diff_gq, diff_gk, diff_lam, diff_g_out, w_out_c, ffn_w_up, ffn_conv_w, ffn_conv_b, ffn_w_down):
    b, s, d = x.shape
    n_ctx = ctx.shape[1]
    l = s + n_ctx
    depth = w_mod.shape[0]
    tr = ROW_TILE
    assert s % tr == 0 and n_ctx == tr and s % GRID_W == 0 and d == DIFF_HEADS * LANES
    nlat, ntiles = s // tr, l // tr

    cc = jnp.concatenate([c, c_ctx[None], jnp.zeros((BF16_ROWS - (b + 1) % BF16_ROWS, d), F32)], axis=0)
    mod = _modulation(cc, w_mod, b_mod)
    mod_lat = mod[:, :b].reshape(depth, b, 1, 6, d)
    mod_ctx = jnp.broadcast_to(mod[:, b].reshape(depth, 1, 1, 6, d), (depth, b, 1, 6, d))
    modtab = jnp.concatenate([mod_lat, mod_ctx], axis=2)

    rope_mla = _rope_tables(s, l, MLA_ROPE // 2,
                            [("id", MLA_NOPE), ("rot",), ("id", LANES - MLA_NOPE - MLA_ROPE)])
    dh = diff_gq.shape[1]
    rope_diff = _rope_tables(s, l, dh // 2, [("rot",), ("rot",)])

    h = jnp.concatenate([x, ctx], axis=1)
    for li in range(depth):
        last = li == depth - 1
        out_tiles = nlat if last else ntiles
        fp = {"w_up": _bf(ffn_w_up[li]), "cw": ffn_conv_w[li], "cb": ffn_conv_b[li][None],
              "w_down": _bf(ffn_w_down[li])}
        g1, g2 = g_norm1[li][None], g_norm2[li][None]
        if li % 2 == 0:
            p = _even_params(li // 2, d, w_in_ab, lru_conv_w, lru_conv_b, lru_gate_w, lru_gate_b,
                             lru_lambda, mla_g_q, mla_g_kv, mla_w_uq, mla_w_ukv, mla_gq_nope,
                             mla_gk_nope, mla_gq_rope, mla_gk_rope, w_out_ab)
            xr, gr, q, k, vt = _even_in(h, modtab[li], g1, p, rope_mla, nlat)
            y1 = _lru(xr, gr, p, s)
            y2 = _attention(q, k, vt, s, out_tiles)
            h = _post(h, y1, y2, 0, modtab[li], g2, p["wo1"], p["wo2"], fp, nlat, out_tiles)
        else:
            p = _odd_params(li // 2, d, w_in_c, diff_gq, diff_gk, diff_lam, diff_g_out, w_out_c)
            lam_init = 0.8 - 0.6 * math.exp(-0.3 * li)
            q, k, vt = _diff_in(h, modtab[li], g1, p, rope_diff, nlat)
            y = _attention(q, k, vt, s, out_tiles, (p["lam_vecs"], p["g_out"]), lam_init)
            h = _post(h, y, y, 1, modtab[li], g2, p["wo1"], p["wo2"], fp, nlat, out_tiles)
    return h
```

```python
import functools
import math

import numpy as np
import jax
import jax.numpy as jnp
from jax import lax
from jax.experimental import pallas as pl
from jax.experimental.pallas import tpu as pltpu

F32 = jnp.float32
BF16 = jnp.bfloat16

EPS = 1e-6
ROPE_THETA = 10000.0
GRID_W = 64
RG_C = 8.0
LRU_BLOCKS = 8
MLA_HEADS = 8
MLA_NOPE = 64
MLA_ROPE = 32
DIFF_HEADS = 8

LANES = 128
SUBLANES = 8
BF16_ROWS = 16
ROW_TILE = 256
LOG2E = 1.4426950408889634
VMEM_LIMIT = 52 * 1024 * 1024


def _bf(x):
    return x.astype(BF16)


def _dot(a, b):
    return jnp.dot(a, b, preferred_element_type=F32)


def _dot_nt(a, b):
    return lax.dot_general(a, b, (((1,), (1,)), ((), ())), preferred_element_type=F32)


def _rms(x, g):
    ms = jnp.mean(x * x, axis=-1, keepdims=True)
    return x * lax.rsqrt(ms + EPS) * g


def _expm1(x):
    u = jnp.exp(x)
    small = jnp.where(u == 1.0, x, (u - 1.0) * x / jnp.log(u))
    return jnp.where(jnp.abs(x) > 1.0, u - 1.0, small)


def _log1p(y):
    w = 1.0 + y
    return jnp.where(w == 1.0, y, jnp.log(w) * y / (w - 1.0))


def _seg_sumsq(x, m):
    x2 = x * x
    hi = _bf(x2)
    lo = _bf(x2 - hi.astype(F32))
    return _dot(hi, m) + _dot(lo, m)


def _heads_norm(x, m, invlen, gain, nheads):
    r = x.shape[0]
    pw = m.shape[0]
    per = pw // LANES
    stacked = jnp.concatenate([x[:, p * pw:(p + 1) * pw] for p in range(nheads // per)], axis=0)
    ss = _seg_sumsq(stacked, m)
    y = stacked * lax.rsqrt(ss * jnp.tile(invlen, (1, per)) + EPS) * jnp.tile(gain, (1, per))
    return [y[(h // per) * r:(h // per + 1) * r, (h % per) * LANES:(h % per + 1) * LANES]
            for h in range(nheads)]


def _rope(x, cos, sina, sinb, shift):
    return x * cos + pltpu.roll(x, LANES - shift, 1) * sina + pltpu.roll(x, shift, 1) * sinb


def _const_spec(shape):
    n = len(shape)
    return pl.BlockSpec(shape, lambda *_: (0,) * n, pipeline_mode=pl.Buffered(1))


def _params(sem):
    return pltpu.CompilerParams(dimension_semantics=sem, vmem_limit_bytes=VMEM_LIMIT)


def _mod_kernel(cc_ref, w_ref, b_ref, o_ref):
    x = cc_ref[...]
    sx = x * jax.nn.sigmoid(x)
    o_ref[0] = _dot(_bf(sx), _bf(w_ref[0])) + b_ref[0]


def _modulation(cc, w_mod, b_mod):
    depth, d, d6 = w_mod.shape
    rows = cc.shape[0]
    return pl.pallas_call(
        _mod_kernel,
        grid=(depth, d6 // d),
        in_specs=[pl.BlockSpec((rows, d), lambda l, j: (0, 0)),
                  pl.BlockSpec((1, d, d), lambda l, j: (l, 0, j)),
                  pl.BlockSpec((1, 1, d), lambda l, j: (l, 0, j))],
        out_specs=pl.BlockSpec((1, rows, d), lambda l, j: (l, 0, j)),
        out_shape=jax.ShapeDtypeStruct((depth, rows, d6), F32),
        compiler_params=_params(("arbitrary", "arbitrary")),
        name="modulation",
    )(cc, w_mod, b_mod.reshape(depth, 1, d6))


def _even_in_kernel(h_ref, mod_ref, g1_ref, win_ref, gq_ref, gkv_ref, wuq_ref, wuk_ref, wuv_ref,
                    m_ref, invlen_ref, gainq_ref, gaink_ref, gkr_ref, cos_ref, sina_ref, sinb_ref,
                    xr_ref, gr_ref, q_ref, k_ref, v_ref, *, d_rnn, q_rank, kv_rank):
    tr = h_ref.shape[1]
    mod = mod_ref[0, 0]
    u = _rms(h_ref[0], g1_ref[...]) * (1.0 + mod[1:2]) + mod[0:1]
    z = _dot(_bf(u), win_ref[...])
    o = 2 * d_rnn
    xr_ref[0] = z[:, :d_rnn]
    gr_ref[0] = z[:, d_rnn:o]
    ql = z[:, o:o + q_rank]
    kvl = z[:, o + q_rank:o + q_rank + kv_rank]
    kr = z[:, o + q_rank + kv_rank:]

    cos, sina, sinb = cos_ref[...], sina_ref[...], sinb_ref[...]
    m, invlen = m_ref[...], invlen_ref[...]

    q = _dot(_bf(_rms(ql, gq_ref[...])), wuq_ref[...])
    qn = _heads_norm(q, m, invlen, gainq_ref[...], MLA_HEADS)
    for h in range(MLA_HEADS):
        q_ref[0, :, h * LANES:(h + 1) * LANES] = _bf(
            _rope(qn[h], cos, sina, sinb, MLA_ROPE // 4))

    kvn = _bf(_rms(kvl, gkv_ref[...]))
    v_ref[0] = _bf(_dot(kvn, wuv_ref[...]))
    kn = _heads_norm(_dot(kvn, wuk_ref[...]), m, invlen, gaink_ref[...], MLA_HEADS)
    ss = jnp.sum(kr * kr, axis=-1, keepdims=True) * (1.0 / MLA_ROPE)
    k_rope = _rope(kr * lax.rsqrt(ss + EPS) * gkr_ref[...], cos, sina, sinb, MLA_ROPE // 4)
    for h in range(MLA_HEADS):
        k_ref[0, :, h * LANES:(h + 1) * LANES] = _bf(kn[h] + k_rope)


def _even_in(h, modtab, g1, p, rope, nlat):
    b, l, d = h.shape
    tr = ROW_TILE
    d_rnn = p["d_rnn"]
    hl = MLA_HEADS * LANES
    hdv = p["w_uv"].shape[1]
    row = lambda w: pl.BlockSpec((1, tr, w), lambda i, t: (i, t, 0))
    tab = pl.BlockSpec((tr, LANES), lambda i, t: (t, 0))
    consts = [g1, p["w_in"], p["g_q"], p["g_kv"], p["w_uq"], p["w_uk"], p["w_uv"],
              p["m"], p["invlen"], p["gainq"], p["gaink"], p["gkr"]]
    kern = functools.partial(_even_in_kernel, d_rnn=d_rnn, q_rank=p["g_q"].shape[1],
                             kv_rank=p["g_kv"].shape[1])
    return pl.pallas_call(
        kern,
        grid=(b, l // tr),
        in_specs=[row(d), pl.BlockSpec((1, 1, 6, d), lambda i, t: (i, t // nlat, 0, 0))]
                 + [_const_spec(c.shape) for c in consts] + [tab, tab, tab],
        out_specs=[row(d_rnn), row(d_rnn), row(hl), row(hl), row(hdv)],
        out_shape=[jax.ShapeDtypeStruct((b, l, d_rnn), F32), jax.ShapeDtypeStruct((b, l, d_rnn), F32),
                   jax.ShapeDtypeStruct((b, l, hl), BF16), jax.ShapeDtypeStruct((b, l, hl), BF16),
                   jax.ShapeDtypeStruct((b, l, hdv), BF16)],
        compiler_params=_params(("parallel", "parallel")),
        name="even_in",
    )(h, modtab, *consts, *rope)


def _scan_chunk(a, b, carry, reverse, row):
    ch = a.shape[0]
    sub = row % SUBLANES
    d = 1
    while d < SUBLANES:
        if not reverse:
            keep = sub >= d
            a_sh = jnp.where(keep, pltpu.roll(a, d, 0), 1.0)
            b_sh = jnp.where(keep, pltpu.roll(b, d, 0), 0.0)
        else:
            keep = sub < SUBLANES - d
            a_sh = jnp.where(keep, pltpu.roll(a, ch - d, 0), 1.0)
            b_sh = jnp.where(keep, pltpu.roll(b, ch - d, 0), 0.0)
        b = a * b_sh + b
        a = a * a_sh
        d *= 2
    ngroups = ch // SUBLANES
    out = [None] * ngroups
    for g in (reversed(range(ngroups)) if reverse else range(ngroups)):
        rows = slice(g * SUBLANES, (g + 1) * SUBLANES)
        hg = b[rows] + a[rows] * carry
        carry = hg[0:1] if reverse else hg[SUBLANES - 1:SUBLANES]
        out[g] = hg
    return jnp.concatenate(out, axis=0), carry


def _lru_kernel(xr_ref, gr_ref, cw_ref, cb_ref, wg_ref, bg_ref, lam_ref, y_ref, xpad, hf, *, s, ch):
    l = xr_ref.shape[1]
    nlat, nctx = s // ch, (l - s) // ch
    conv_w = cw_ref.shape[0]
    ext_rows = ch + 16
    xpad[0:8, :] = jnp.zeros((8, LANES), F32)
    xpad[8:8 + l, :] = xr_ref[0]
    xpad[8 + l:16 + l, :] = jnp.zeros((8, LANES), F32)
    cw, cb = cw_ref[...], cb_ref[...]
    nl = -lam_ref[0]
    sp = jnp.maximum(nl, 0.0) + _log1p(jnp.exp(-jnp.abs(nl)))
    row = lax.broadcasted_iota(jnp.int32, (ch, LANES), 0)

    def chunk(r0, direction, carry, in_ctx):
        ext = xpad[pl.ds(r0, ext_rows), :]
        x = jnp.zeros((ch, LANES), F32) + cb
        for w in range(conv_w):
            xs = pltpu.roll(ext, ext_rows - (7 + w), 0)[0:ch]
            src = r0 + row + (w - 1)
            if in_ctx and w == 0:
                xs = jnp.where(src >= s, xs, 0.0)
            if not in_ctx and w >= 2:
                xs = jnp.where(src < s, xs, 0.0)
            x = x + xs * cw[w:w + 1]
        c0 = direction * 2 * LANES
        pre = _dot(_bf(x), wg_ref[0, :, c0:c0 + 2 * LANES]) + bg_ref[0, :, c0:c0 + 2 * LANES]
        r = jax.nn.sigmoid(pre[:, :LANES])
        i = jax.nn.sigmoid(pre[:, LANES:])
        log_a = -RG_C * r * sp[direction:direction + 1]
        a = jnp.exp(log_a)
        bb = jnp.sqrt(-_expm1(2.0 * log_a)) * (i * x)
        return _scan_chunk(a, bb, carry, direction == 1, row)

    carry = jnp.zeros((1, LANES), F32)
    for c in range(nctx):
        r0 = s + c * ch
        hh, carry = chunk(r0, 0, carry, True)
        hf[r0:r0 + ch, :] = hh

    def fwd(i, carry):
        r0 = pl.multiple_of(i * ch, ch)
        hh, carry = chunk(r0, 0, carry, False)
        hf[pl.ds(r0, ch), :] = hh
        return carry

    lax.fori_loop(0, nlat, fwd, carry)

    def emit(r0, hb):
        rows = pl.ds(r0, ch)
        y_ref[0, rows, :] = _bf((hf[rows, :] + hb) * jax.nn.gelu(gr_ref[0, rows, :]))

    carry = jnp.zeros((1, LANES), F32)
    for c in reversed(range(nctx)):
        r0 = s + c * ch
        hb, carry = chunk(r0, 1, carry, True)
        emit(r0, hb)

    def bwd(i, carry):
        r0 = pl.multiple_of((nlat - 1 - i) * ch, ch)
        hb, carry = chunk(r0, 1, carry, False)
        emit(r0, hb)
        return carry

    lax.fori_loop(0, nlat, bwd, carry)


def _lru(xr, gr, p, s):
    b, l, d_rnn = xr.shape
    ng = d_rnn // LANES
    col = pl.BlockSpec((1, l, LANES), lambda i, j: (i, 0, j))
    return pl.pallas_call(
        functools.partial(_lru_kernel, s=s, ch=ROW_TILE),
        grid=(b, ng),
        in_specs=[col, col,
                  pl.BlockSpec((p["lru_cw"].shape[0], LANES), lambda i, j: (0, j)),
                  pl.BlockSpec((1, LANES), lambda i, j: (0, j)),
                  pl.BlockSpec((1, LANES, 4 * LANES), lambda i, j: (j, 0, 0)),
                  pl.BlockSpec((1, 1, 4 * LANES), lambda i, j: (j, 0, 0)),
                  pl.BlockSpec((1, 2, LANES), lambda i, j: (j, 0, 0))],
        out_specs=col,
        out_shape=jax.ShapeDtypeStruct((b, l, d_rnn), BF16),
        scratch_shapes=[pltpu.VMEM((l + 16, LANES), F32), pltpu.VMEM((l, LANES), F32)],
        compiler_params=_params(("parallel", "parallel")),
        name="rglru",
    )(xr, gr, p["lru_cw"], p["lru_cb"], p["lru_wg"], p["lru_bg"], p["lru_lam"])


def _mla_attn_kernel(q_ref, k_ref, v_ref, o_ref, *, s, nlat):
    tq = q_ref.shape[1]
    l = k_ref.shape[1]
    lane = lax.broadcasted_iota(jnp.int32, (tq, LANES), 1)

    def run(kv0, kvn):
        for hp in range(MLA_HEADS // 2):
            vpair = v_ref[0, kv0:kv0 + kvn, hp * LANES:(hp + 1) * LANES]
            outs = []
            for sub in range(2):
                c0 = (2 * hp + sub) * LANES
                sc = _dot_nt(q_ref[0, :, c0:c0 + LANES], k_ref[0, kv0:kv0 + kvn, c0:c0 + LANES])
                e = jnp.exp2(sc - jnp.max(sc, axis=-1, keepdims=True))
                den = jnp.sum(e, axis=-1, keepdims=True)
                outs.append(_dot(_bf(e), vpair) / den)
            o_ref[0, :, hp * LANES:(hp + 1) * LANES] = _bf(
                jnp.where(lane < LANES // 2, outs[0], outs[1]))

    t = pl.program_id(1)

    @pl.when(t < nlat)
    def _():
        run(0, l)

    @pl.when(t >= nlat)
    def _():
        run(s, l - s)


def _mla_attn(q, k, v, s, nq_tiles):
    b, l, hl = q.shape
    tq = ROW_TILE
    dv = v.shape[2]
    full = lambda w: pl.BlockSpec((1, l, w), lambda i, t: (i, 0, 0), pipeline_mode=pl.Buffered(1))
    return pl.pallas_call(
        functools.partial(_mla_attn_kernel, s=s, nlat=s // tq),
        grid=(b, nq_tiles),
        in_specs=[pl.BlockSpec((1, tq, hl), lambda i, t: (i, t, 0)), full(hl), full(dv)],
        out_specs=pl.BlockSpec((1, tq, dv), lambda i, t: (i, t, 0)),
        out_shape=jax.ShapeDtypeStruct((b, l, dv), BF16),
        compiler_params=_params(("parallel", "arbitrary")),
        name="mla_attn",
    )(q, k, v)


def _diff_attn_kernel(q_ref, k_ref, vt_ref, lamv_ref, gout_ref, o_ref, s_buf, e_buf, *, s, lam_init):
    tq = q_ref.shape[1]
    nchunks, hdv, kc = vt_ref.shape[1:]
    nheads = q_ref.shape[2] // LANES
    dv = hdv // nheads
    grp = kc // 8
    subs = range(2)
    lane = lax.broadcasted_iota(jnp.int32, (tq, LANES), 1)
    lv = lamv_ref[...]
    lam = (jnp.exp(jnp.sum(lv[0:1] * lv[1:2], axis=-1, keepdims=True))
           - jnp.exp(jnp.sum(lv[2:3] * lv[3:4], axis=-1, keepdims=True)) + lam_init)
    gout = gout_ref[...]

    def q_operands(h):
        qh = q_ref[0, :, h * LANES:(h + 1) * LANES]
        zq = jnp.zeros_like(qh)
        first = lane < LANES // 2
        return [jnp.where(first, qh, zq), jnp.where(first, zq, qh)]

    def fold(x, op):
        return op(x.reshape(grp, 8, tq), axis=0)

    def run(j_lo, j_hi):
        mrun = lsum = None
        for t in range(nheads + 2):
            ha, hb, hc = t, t - 1, t - 2
            do_a, do_b, do_c = ha < nheads, 0 <= hb < nheads, 0 <= hc < nheads
            if do_a:
                qs = q_operands(ha)
            if do_c:
                ls = [jnp.sum(lsum[i], axis=0, keepdims=True) for i in subs]
                rinv = 1.0 / ls[0]
                cb = jnp.broadcast_to(_bf(lam * ls[0] / ls[1]), (kc, tq))
            if do_b:
                m8 = [jnp.broadcast_to(jnp.max(mrun[i], axis=0, keepdims=True), (8, tq)) for i in subs]
            mrun = [jnp.full((8, tq), -jnp.inf, F32) for _ in subs]
            lsum = [jnp.zeros((8, tq), F32) for _ in subs]
            oacc = jnp.zeros((dv, tq), F32)
            for j in range(j_lo, j_hi):
                if do_b:
                    sb = [s_buf[i, j] for i in subs]
                if do_c:
                    eb = [e_buf[i, j] for i in subs]
                if do_a:
                    kj = k_ref[0, j * kc:(j + 1) * kc, ha * LANES:(ha + 1) * LANES]
                    sc = [_dot_nt(kj, qs[i]) for i in subs]
                    for i in subs:
                        s_buf[i, j] = sc[i]
                    mrun = [jnp.maximum(mrun[i], fold(sc[i], jnp.max)) for i in subs]
                if do_b:
                    e = [jnp.exp2(sb[i].reshape(grp, 8, tq) - m8[i]) for i in subs]
                    lsum = [lsum[i] + jnp.sum(e[i], axis=0) for i in subs]
                    for i in subs:
                        e_buf[i, j] = _bf(e[i].reshape(kc, tq))
                if do_c:
                    oacc = oacc + _dot(vt_ref[0, j, hc * dv:(hc + 1) * dv, :], eb[0] - eb[1] * cb)
            if do_c:
                o = _rms((oacc * rinv).T, gout) * (1.0 - lam_init)
                o_ref[0, :, hc * dv:(hc + 1) * dv] = _bf(o)

    t = pl.program_id(1)
    nlat = s // tq

    @pl.when(t < nlat)
    def _():
        run(0, nchunks)

    @pl.when(t >= nlat)
    def _():
        run(s // kc, nchunks)


def _diff_attn(q, k, vt, lam_vecs, g_out, s, nq_tiles, lam_init):
    b, l, hl = q.shape
    tq = ROW_TILE
    nchunks, hdv, kc = vt.shape[1:]
    whole = lambda a: pl.BlockSpec((1,) + a.shape[1:], lambda i, t: (i,) + (0,) * (a.ndim - 1),
                                   pipeline_mode=pl.Buffered(1))
    return pl.pallas_call(
        functools.partial(_diff_attn_kernel, s=s, lam_init=lam_init),
        grid=(b, nq_tiles),
        in_specs=[pl.BlockSpec((1, tq, hl), lambda i, t: (i, t, 0)), whole(k), whole(vt),
                  _const_spec(lam_vecs.shape), _const_spec(g_out.shape)],
        out_specs=pl.BlockSpec((1, tq, hdv), lambda i, t: (i, t, 0)),
        out_shape=jax.ShapeDtypeStruct((b, l, hdv), BF16),
        scratch_shapes=[pltpu.VMEM((2, nchunks, kc, tq), F32), pltpu.VMEM((2, nchunks, kc, tq), BF16)],
        compiler_params=_params(("parallel", "arbitrary")),
        name="diff_attn",
    )(q, k, vt, lam_vecs, g_out)


def _diff_in_kernel(h_ref, mod_ref, g1_ref, win_ref, m_ref, invlen_ref, gainq_ref, gaink_ref,
                    cos_ref, sina_ref, sinb_ref, q_ref, k_ref, vt_ref, *, dh):
    tr = h_ref.shape[1]
    d = h_ref.shape[2]
    mod = mod_ref[0, 0]
    u = _rms(h_ref[0], g1_ref[...]) * (1.0 + mod[1:2]) + mod[0:1]
    z = _dot(_bf(u), win_ref[...])
    vt_ref[0, 0] = _bf(z[:, 2 * d:].T)
    cos, sina, sinb = cos_ref[...], sina_ref[...], sinb_ref[...]
    m, invlen = m_ref[...], invlen_ref[...]
    for src, gain_ref, dst in ((z[:, :d], gainq_ref, q_ref), (z[:, d:2 * d], gaink_ref, k_ref)):
        xn = _heads_norm(src, m, invlen, gain_ref[...], DIFF_HEADS)
        for h in range(DIFF_HEADS):
            dst[0, :, h * LANES:(h + 1) * LANES] = _bf(
                _rope(xn[h], cos, sina, sinb, dh // 4))


def _diff_in(h, modtab, g1, p, rope, nlat):
    b, l, d = h.shape
    tr = ROW_TILE
    row = pl.BlockSpec((1, tr, d), lambda i, t: (i, t, 0))
    tab = pl.BlockSpec((tr, LANES), lambda i, t: (t, 0))
    consts = [g1, p["w_in"], p["m"], p["invlen"], p["gainq"], p["gaink"]]
    return pl.pallas_call(
        functools.partial(_diff_in_kernel, dh=p["dh"]),
        grid=(b, l // tr),
        in_specs=[row, pl.BlockSpec((1, 1, 6, d), lambda i, t: (i, t // nlat, 0, 0))]
                 + [_const_spec(c.shape) for c in consts] + [tab, tab, tab],
        out_specs=[row, row, pl.BlockSpec((1, 1, d, tr), lambda i, t: (i, t, 0, 0))],
        out_shape=[jax.ShapeDtypeStruct((b, l, d), BF16), jax.ShapeDtypeStruct((b, l, d), BF16),
                   jax.ShapeDtypeStruct((b, l // tr, d, tr), BF16)],
        compiler_params=_params(("parallel", "parallel")),
        name="diff_in",
    )(h, modtab, *consts, *rope)


def _post_kernel(hm_ref, hp_ref, hn_ref, y1m_ref, y1p_ref, y1n_ref, y2m_ref, y2p_ref, y2n_ref,
                 mod_ref, g2_ref, wo1_ref, wo2_ref, wup_ref, cw_ref, cb_ref, wdn_ref, o_ref,
                 *, nlat, ntiles, ff_chunk):
    tr = hm_ref.shape[1]
    halo = hp_ref.shape[1]
    ext = tr + 2 * halo
    d_ff = wdn_ref.shape[0]
    mod = mod_ref[0, 0]
    cat = lambda p, m, n: jnp.concatenate([p[0], m[0], n[0]], axis=0)
    y = _dot(cat(y1p_ref, y1m_ref, y1n_ref), wo1_ref[...]) + _dot(cat(y2p_ref, y2m_ref, y2n_ref), wo2_ref[...])
    h1 = cat(hp_ref, hm_ref, hn_ref) + mod[2:3] * y
    u = _bf(_rms(h1, g2_ref[...]) * (1.0 + mod[4:5]) + mod[3:4])

    t = pl.program_id(1)
    row = lax.broadcasted_iota(jnp.int32, (tr, 1), 0)
    seg_start = jnp.logical_or(t == 0, t == nlat)
    seg_end = jnp.logical_or(t == nlat - 1, t == ntiles - 1)
    keep_prev = jnp.logical_not(jnp.logical_and(seg_start, row == 0))
    keep_next = jnp.logical_not(jnp.logical_and(seg_end, row == tr - 1))

    acc = jnp.zeros((tr, hm_ref.shape[2]), F32)
    for c0 in range(0, d_ff, ff_chunk):
        a = _dot(u[halo:halo + tr], wup_ref[:, c0:c0 + ff_chunk])
        g = _dot(u, wup_ref[:, d_ff + c0:d_ff + c0 + ff_chunk])
        cw = cw_ref[:, c0:c0 + ff_chunk]
        g_prev = jnp.where(keep_prev, pltpu.roll(g, 1, 0)[halo:halo + tr], 0.0)
        g_next = jnp.where(keep_next, pltpu.roll(g, ext - 1, 0)[halo:halo + tr], 0.0)
        gc = (cb_ref[:, c0:c0 + ff_chunk] + g_prev * cw[0:1] + g[halo:halo + tr] * cw[1:2]
              + g_next * cw[2:3])
        acc = acc + _dot(_bf(jax.nn.gelu(gc) * a), wdn_ref[c0:c0 + ff_chunk, :])
    o_ref[0] = h1[halo:halo + tr] + mod[5:6] * acc


def _post(h, y1, y2, y2_col, modtab, g2, wo1, wo2, fp, nlat, out_tiles):
    b, l, d = h.shape
    tr, halo = ROW_TILE, BF16_ROWS
    ntiles = l // tr
    per = tr // halo
    last = out_tiles * per - 1
    yw = wo1.shape[0]
    main = lambda w, c: pl.BlockSpec((1, tr, w), lambda i, t: (i, t, c))
    prev = lambda w, c: pl.BlockSpec((1, halo, w), lambda i, t: (i, jnp.maximum(t * per - 1, 0), c))
    nxt = lambda w, c: pl.BlockSpec((1, halo, w), lambda i, t: (i, jnp.minimum((t + 1) * per, last), c))
    consts = [g2, wo1, wo2, fp["w_up"], fp["cw"], fp["cb"], fp["w_down"]]
    d_ff = fp["w_down"].shape[0]
    kern = functools.partial(_post_kernel, nlat=nlat, ntiles=ntiles, ff_chunk=d_ff // 2)
    return pl.pallas_call(
        kern,
        grid=(b, out_tiles),
        in_specs=[main(d, 0), prev(d, 0), nxt(d, 0),
                  main(yw, 0), prev(yw, 0), nxt(yw, 0),
                  main(yw, y2_col), prev(yw, y2_col), nxt(yw, y2_col),
                  pl.BlockSpec((1, 1, 6, d), lambda i, t: (i, t // nlat, 0, 0))]
                 + [_const_spec(c.shape) for c in consts],
        out_specs=main(d, 0),
        out_shape=jax.ShapeDtypeStruct((b, out_tiles * tr, d), F32),
        compiler_params=_params(("parallel", "parallel")),
        name="post",
    )(h, h, h, y1, y1, y1, y2, y2, y2, modtab, *consts)


def _rope_tables(s, l, da, blocks):
    half = da // 2
    rows = s // GRID_W
    row = jnp.repeat(jnp.arange(rows, dtype=F32), GRID_W)
    col = jnp.tile(jnp.arange(GRID_W, dtype=F32), rows)
    inv = ROPE_THETA ** (-jnp.arange(0, da, 2, dtype=F32) / da)
    ar, ac = row[:, None] * inv, col[:, None] * inv
    cr, sr, cc, sc = jnp.cos(ar), jnp.sin(ar), jnp.cos(ac), jnp.sin(ac)
    z = jnp.zeros((s, half), F32)
    cos_p, sina_p, sinb_p = [], [], []
    for blk in blocks:
        if blk[0] == "id":
            cos_p.append(jnp.ones((s, blk[1]), F32))
            sina_p.append(jnp.zeros((s, blk[1]), F32))
            sinb_p.append(jnp.zeros((s, blk[1]), F32))
        else:
            cos_p += [cr, cr, cc, cc]
            sina_p += [-sr, z, -sc, z]
            sinb_p += [z, sr, z, sc]
    pad = lambda x, v: jnp.concatenate([x, jnp.full((l - s, LANES), v, F32)], axis=0)
    return (pad(jnp.concatenate(cos_p, axis=1), 1.0), pad(jnp.concatenate(sina_p, axis=1), 0.0),
            pad(jnp.concatenate(sinb_p, axis=1), 0.0))


def _seg_matrix(seg_ids):
    ids = np.asarray(seg_ids)
    return jnp.asarray((ids[:, None] == ids[None, :]).astype(np.float32), BF16)


def _lanes(*parts):
    return jnp.concatenate([jnp.asarray(p, F32).reshape(-1) for p in parts]).reshape(1, -1)


def _even_params(e, d, w_in_ab, lru_conv_w, lru_conv_b, lru_gate_w, lru_gate_b, lru_lambda, mla_g_q,
                 mla_g_kv, mla_w_uq, mla_w_ukv, gqn, gkn, gqr, gkr, w_out_ab):
    d_rnn = lru_conv_w.shape[2]
    q_rank, kv_rank = mla_g_q.shape[1], mla_g_kv.shape[1]
    nope, rope, hh = MLA_NOPE, MLA_ROPE, MLA_HEADS
    dv = mla_w_ukv.shape[2] // hh - nope
    w_in = w_in_ab[e]
    base = 2 * d_rnn + q_rank + kv_rank
    w_in_pad = jnp.concatenate([w_in[:, :base], jnp.zeros((d, nope), F32), w_in[:, base:],
                                jnp.zeros((d, LANES - nope - rope), F32)], axis=1)
    w_uq = mla_w_uq[e].reshape(q_rank, hh, nope + rope)
    w_uq = jnp.pad(w_uq, ((0, 0), (0, 0), (0, LANES - nope - rope))).reshape(q_rank, hh * LANES)
    w_ukv = mla_w_ukv[e].reshape(kv_rank, hh, nope + dv)
    w_uk = jnp.pad(w_ukv[:, :, :nope], ((0, 0), (0, 0), (0, LANES - nope))).reshape(kv_rank, hh * LANES)
    w_uv = w_ukv[:, :, nope:].reshape(kv_rank, hh * dv)
    scale = (nope + rope) ** -0.5 * LOG2E
    rest = LANES - nope - rope
    gw = lru_gate_w[e]
    bs = gw.shape[-1]
    per = LANES // bs
    ng = d_rnn // LANES
    gwg = gw.reshape(2, 2, ng, per, bs, bs)
    eye = jnp.eye(per, dtype=F32)
    wg = jnp.einsum("dgjpcn,pq->jpcdgqn", gwg, eye).reshape(ng, LANES, 4 * LANES)
    bg = lru_gate_b[e].reshape(2, 2, ng, LANES).transpose(2, 0, 1, 3).reshape(ng, 1, 4 * LANES)
    lam = lru_lambda[e].reshape(2, ng, LANES).transpose(1, 0, 2)
    return {
        "d_rnn": d_rnn,
        "w_in": _bf(w_in_pad), "g_q": mla_g_q[e][None], "g_kv": mla_g_kv[e][None],
        "w_uq": _bf(w_uq), "w_uk": _bf(w_uk), "w_uv": _bf(w_uv),
        "m": _seg_matrix(([0] * nope + [1] * rope + [2] * rest) + ([3] * nope + [4] * rope + [5] * rest)),
        "invlen": _lanes(np.full(nope, 1.0 / nope), np.full(LANES - nope, 1.0 / rope)),
        "gainq": _lanes(gqn[e] * scale, gqr[e] * scale, np.zeros(rest)),
        "gaink": _lanes(gkn[e], np.zeros(LANES - nope)),
        "gkr": _lanes(np.zeros(nope), gkr[e], np.zeros(rest)),
        "lru_cw": lru_conv_w[e], "lru_cb": lru_conv_b[e][None],
        "lru_wg": _bf(wg), "lru_bg": bg, "lru_lam": lam,
        "wo1": _bf(w_out_ab[e][:d_rnn]), "wo2": _bf(w_out_ab[e][d_rnn:]),
    }


def _odd_params(o, d, w_in_c, diff_gq, diff_gk, diff_lam, diff_g_out, w_out_c):
    dh = diff_gq.shape[1]
    scale = dh ** -0.5 * LOG2E
    return {
        "dh": dh,
        "w_in": _bf(w_in_c[o]),
        "m": _seg_matrix([0] * dh + [1] * dh + [2] * dh + [3] * dh),
        "invlen": _lanes(np.full(LANES, 1.0 / dh)),
        "gainq": _lanes(diff_gq[o] * scale, diff_gq[o] * scale),
        "gaink": _lanes(diff_gk[o], diff_gk[o]),
        "lam_vecs": diff_lam[o], "g_out": diff_g_out[o][None],
        "wo1": _bf(w_out_c[o][:d // 2]), "wo2": _bf(w_out_c[o][d // 2:]),
    }


def kernel(x, c, ctx, c_ctx, w_mod, b_mod, g_norm1, g_norm2, w_in_ab, lru_conv_w, lru_conv_b, lru_gate_w, lru_gate_b, lru_lambda, mla_g_q, mla_g_kv, mla_w_uq, mla_w_ukv, mla_gq_nope, mla_gk_nope, mla_gq_rope, mla_gk_rope, w_out_ab, w_in_c, diff_gq, diff_gk, diff_lam, diff_g_out, w_out_c, ffn_w_up, ffn_conv_w, ffn_conv_b, ffn_w_down):
    b, s, d = x.shape
    n_ctx = ctx.shape[1]
    l = s + n_ctx
    depth = w_mod.shape[0]
    tr = ROW_TILE
    assert s % tr == 0 and n_ctx == tr and s % GRID_W == 0 and d == DIFF_HEADS * LANES
    nlat, ntiles = s // tr, l // tr

    cc = jnp.concatenate([c, c_ctx[None], jnp.zeros((BF16_ROWS - (b + 1) % BF16_ROWS, d), F32)], axis=0)
    mod = _modulation(cc, w_mod, b_mod)
    mod_lat = mod[:, :b].reshape(depth, b, 1, 6, d)
    mod_ctx = jnp.broadcast_to(mod[:, b].reshape(depth, 1, 1, 6, d), (depth, b, 1, 6, d))
    modtab = jnp.concatenate([mod_lat, mod_ctx], axis=2)

    rope_mla = _rope_tables(s, l, MLA_ROPE // 2,
                            [("id", MLA_NOPE), ("rot",), ("id", LANES - MLA_NOPE - MLA_ROPE)])
    dh = diff_gq.shape[1]
    rope_diff = _rope_tables(s, l, dh // 2, [("rot",), ("rot",)])

    h = jnp.concatenate([x, ctx], axis=1)
    for li in range(depth):
        last = li == depth - 1
        out_tiles = nlat if last else ntiles
        fp = {"w_up": _bf(ffn_w_up[li]), "cw": ffn_conv_w[li], "cb": ffn_conv_b[li][None],
              "w_down": _bf(ffn_w_down[li])}
        g1, g2 = g_norm1[li][None], g_norm2[li][None]
        if li % 2 == 0:
            p = _even_params(li // 2, d, w_in_ab, lru_conv_w, lru_conv_b, lru_gate_w, lru_gate_b,
                             lru_lambda, mla_g_q, mla_g_kv, mla_w_uq, mla_w_ukv, mla_gq_nope,
                             mla_gk_nope, mla_gq_rope, mla_gk_rope, w_out_ab)
            xr, gr, q, k, v = _even_in(h, modtab[li], g1, p, rope_mla, nlat)
            y1 = _lru(xr, gr, p, s)
            y2 = _mla_attn(q, k, v, s, out_tiles)
            h = _post(h, y1, y2, 0, modtab[li], g2, p["wo1"], p["wo2"], fp, nlat, out_tiles)
        else:
            p = _odd_params(li // 2, d, w_in_c, diff_gq, diff_gk, diff_lam, diff_g_out, w_out_c)
            lam_init = 0.8 - 0.6 * math.exp(-0.3 * li)
            q, k, vt = _diff_in(h, modtab[li], g1, p, rope_diff, nlat)
            y = _diff_attn(q, k, vt, p["lam_vecs"], p["g_out"], s, out_tiles, lam_init)
            h = _post(h, y, y, 1, modtab[li], g2, p["wo1"], p["wo2"], fp, nlat, out_tiles)
    return h
```

```python
import functools
import math

import numpy as np
import jax
import jax.numpy as jnp
from jax import lax
from jax.experimental import pallas as pl
from jax.experimental.pallas import tpu as pltpu

F32 = jnp.float32
BF16 = jnp.bfloat16

EPS = 1e-6
ROPE_THETA = 10000.0
GRID_W = 64
RG_C = 8.0
LRU_BLOCKS = 8
MLA_HEADS = 8
MLA_NOPE = 64
MLA_ROPE = 32
DIFF_HEADS = 8

LANES = 128
SUBLANES = 8
BF16_ROWS = 16
ROW_TILE = 256
LRU_UNROLL = 2
LOG2E = 1.4426950408889634
VMEM_LIMIT = 52 * 1024 * 1024


def _bf(x):
    return x.astype(BF16)


def _dot(a, b):
    return jnp.dot(a, b, preferred_element_type=F32)


def _dot_nt(a, b):
    return lax.dot_general(a, b, (((1,), (1,)), ((), ())), preferred_element_type=F32)


def _rms(x, g):
    ms = jnp.mean(x * x, axis=-1, keepdims=True)
    return x * lax.rsqrt(ms + EPS) * g


def _expm1(x, u):
    small = jnp.where(u == 1.0, x, (u - 1.0) * x / jnp.log(u))
    return jnp.where(jnp.abs(x) > 1.0, u - 1.0, small)


def _log1p(y):
    w = 1.0 + y
    return jnp.where(w == 1.0, y, jnp.log(w) * y / (w - 1.0))


def _seg_sumsq(x, m):
    x2 = x * x
    hi = _bf(x2)
    lo = _bf(x2 - hi.astype(F32))
    return _dot(hi, m) + _dot(lo, m)


def _heads_norm(x, m, invlen, gain, nheads):
    r = x.shape[0]
    pw = m.shape[0]
    per = pw // LANES
    stacked = jnp.concatenate([x[:, p * pw:(p + 1) * pw] for p in range(nheads // per)], axis=0)
    ss = _seg_sumsq(stacked, m)
    y = stacked * lax.rsqrt(ss * jnp.tile(invlen, (1, per)) + EPS) * jnp.tile(gain, (1, per))
    return [y[(h // per) * r:(h // per + 1) * r, (h % per) * LANES:(h % per + 1) * LANES]
            for h in range(nheads)]


def _rope(x, cos, sina, sinb, shift):
    return x * cos + pltpu.roll(x, LANES - shift, 1) * sina + pltpu.roll(x, shift, 1) * sinb


def _const_spec(shape):
    n = len(shape)
    return pl.BlockSpec(shape, lambda *_: (0,) * n, pipeline_mode=pl.Buffered(1))


def _params(sem):
    return pltpu.CompilerParams(dimension_semantics=sem, vmem_limit_bytes=VMEM_LIMIT)


def _mod_kernel(cc_ref, w_ref, b_ref, o_ref):
    x = cc_ref[...]
    sx = x * jax.nn.sigmoid(x)
    o_ref[0] = _dot(_bf(sx), _bf(w_ref[0])) + b_ref[0]


def _modulation(cc, w_mod, b_mod):
    depth, d, d6 = w_mod.shape
    rows = cc.shape[0]
    return pl.pallas_call(
        _mod_kernel,
        grid=(depth, d6 // d),
        in_specs=[pl.BlockSpec((rows, d), lambda l, j: (0, 0)),
                  pl.BlockSpec((1, d, d), lambda l, j: (l, 0, j)),
                  pl.BlockSpec((1, 1, d), lambda l, j: (l, 0, j))],
        out_specs=pl.BlockSpec((1, rows, d), lambda l, j: (l, 0, j)),
        out_shape=jax.ShapeDtypeStruct((depth, rows, d6), F32),
        compiler_params=_params(("arbitrary", "arbitrary")),
        name="modulation",
    )(cc, w_mod, b_mod.reshape(depth, 1, d6))


def _even_in_kernel(h_ref, mod_ref, g1_ref, win_ref, gq_ref, gkv_ref, wuq_ref, wuk_ref, wuv_ref,
                    m_ref, invlen_ref, gainq_ref, gaink_ref, gkr_ref, cos_ref, sina_ref, sinb_ref,
                    xr_ref, gr_ref, q_ref, k_ref, vt_ref, *, d_rnn, q_rank, kv_rank):
    tr = h_ref.shape[1]
    mod = mod_ref[0, 0]
    u = _rms(h_ref[0], g1_ref[...]) * (1.0 + mod[1:2]) + mod[0:1]
    z = _dot(_bf(u), win_ref[...])
    o = 2 * d_rnn
    xr_ref[0] = z[:, :d_rnn]
    gr_ref[0] = z[:, d_rnn:o]
    ql = z[:, o:o + q_rank]
    kvl = z[:, o + q_rank:o + q_rank + kv_rank]
    kr = z[:, o + q_rank + kv_rank:]

    cos, sina, sinb = cos_ref[...], sina_ref[...], sinb_ref[...]
    m, invlen = m_ref[...], invlen_ref[...]

    q = _dot(_bf(_rms(ql, gq_ref[...])), wuq_ref[...])
    qn = _heads_norm(q, m, invlen, gainq_ref[...], MLA_HEADS)
    for h in range(MLA_HEADS):
        q_ref[0, :, h * LANES:(h + 1) * LANES] = _bf(
            _rope(qn[h], cos, sina, sinb, MLA_ROPE // 4))

    kvn = _bf(_rms(kvl, gkv_ref[...]))
    vt_ref[0, 0] = _bf(_dot(kvn, wuv_ref[...]).T)
    kn = _heads_norm(_dot(kvn, wuk_ref[...]), m, invlen, gaink_ref[...], MLA_HEADS)
    ss = jnp.sum(kr * kr, axis=-1, keepdims=True) * (1.0 / MLA_ROPE)
    k_rope = _rope(kr * lax.rsqrt(ss + EPS) * gkr_ref[...], cos, sina, sinb, MLA_ROPE // 4)
    for h in range(MLA_HEADS):
        k_ref[0, :, h * LANES:(h + 1) * LANES] = _bf(kn[h] + k_rope)


def _even_in(h, modtab, g1, p, rope, nlat):
    b, l, d = h.shape
    tr = ROW_TILE
    d_rnn = p["d_rnn"]
    hl = MLA_HEADS * LANES
    hdv = p["w_uv"].shape[1]
    row = lambda w: pl.BlockSpec((1, tr, w), lambda i, t: (i, t, 0))
    tab = pl.BlockSpec((tr, LANES), lambda i, t: (t, 0))
    consts = [g1, p["w_in"], p["g_q"], p["g_kv"], p["w_uq"], p["w_uk"], p["w_uv"],
              p["m"], p["invlen"], p["gainq"], p["gaink"], p["gkr"]]
    kern = functools.partial(_even_in_kernel, d_rnn=d_rnn, q_rank=p["g_q"].shape[1],
                             kv_rank=p["g_kv"].shape[1])
    return pl.pallas_call(
        kern,
        grid=(b, l // tr),
        in_specs=[row(d), pl.BlockSpec((1, 1, 6, d), lambda i, t: (i, t // nlat, 0, 0))]
                 + [_const_spec(c.shape) for c in consts] + [tab, tab, tab],
        out_specs=[row(d_rnn), row(d_rnn), row(hl), row(hl),
                   pl.BlockSpec((1, 1, hdv, tr), lambda i, t: (i, t, 0, 0))],
        out_shape=[jax.ShapeDtypeStruct((b, l, d_rnn), F32), jax.ShapeDtypeStruct((b, l, d_rnn), F32),
                   jax.ShapeDtypeStruct((b, l, hl), BF16), jax.ShapeDtypeStruct((b, l, hl), BF16),
                   jax.ShapeDtypeStruct((b, l // tr, hdv, tr), BF16)],
        compiler_params=_params(("parallel", "parallel")),
        name="even_in",
    )(h, modtab, *consts, *rope)


def _doubling_scan(a, b, pos, reverse):
    axis = a.ndim - 2
    n = a.shape[axis]
    d = 1
    while d < n:
        if not reverse:
            keep = pos >= d
            a_sh = jnp.where(keep, pltpu.roll(a, d, axis), 1.0)
            b_sh = jnp.where(keep, pltpu.roll(b, d, axis), 0.0)
        else:
            keep = pos < n - d
            a_sh = jnp.where(keep, pltpu.roll(a, n - d, axis), 1.0)
            b_sh = jnp.where(keep, pltpu.roll(b, n - d, axis), 0.0)
        b = a * b_sh + b
        a = a * a_sh
        d *= 2
    return a, b


def _scan_chunk(a, b, carry, reverse, grow, ab_scr):
    ch = a.shape[0]
    ngroups = ch // SUBLANES
    grouped = (ngroups, SUBLANES, LANES)
    a, b = _doubling_scan(a.reshape(grouped), b.reshape(grouped),
                          lax.broadcasted_iota(jnp.int32, grouped, 1), reverse)
    a, b = a.reshape(ch, LANES), b.reshape(ch, LANES)
    ab_scr[0] = a
    ab_scr[1] = b
    last = 0 if reverse else SUBLANES - 1
    ga = ab_scr[0, pl.ds(last, ngroups, stride=SUBLANES), :]
    gb = ab_scr[1, pl.ds(last, ngroups, stride=SUBLANES), :]
    ga, gb = _doubling_scan(ga, gb, grow, reverse)
    leaving = gb + ga * carry
    if not reverse:
        entering = jnp.where(grow >= 1, pltpu.roll(leaving, 1, 0), carry)
        carry = leaving[ngroups - 1:ngroups]
    else:
        entering = jnp.where(grow < ngroups - 1, pltpu.roll(leaving, ngroups - 1, 0), carry)
        carry = leaving[0:1]
    out = [b[g * SUBLANES:(g + 1) * SUBLANES] + a[g * SUBLANES:(g + 1) * SUBLANES] * entering[g:g + 1]
           for g in range(ngroups)]
    return jnp.concatenate(out, axis=0), carry


def _lru_kernel(xr_ref, gr_ref, cw_ref, cb_ref, wg_ref, bg_ref, lam_ref, y_ref, xpad, xc, hf, ab_scr,
                *, s, ch):
    l = xr_ref.shape[1]
    nlat, nctx = s // ch, (l - s) // ch
    conv_w = cw_ref.shape[0]
    ext_rows = ch + 16
    xpad[0:8, :] = jnp.zeros((8, LANES), F32)
    xpad[8:8 + l, :] = xr_ref[0]
    xpad[8 + l:16 + l, :] = jnp.zeros((8, LANES), F32)
    cw, cb = cw_ref[...], cb_ref[...]
    nl = -lam_ref[0]
    sp = jnp.maximum(nl, 0.0) + _log1p(jnp.exp(-jnp.abs(nl)))
    row = lax.broadcasted_iota(jnp.int32, (ch, LANES), 0)
    grow = lax.broadcasted_iota(jnp.int32, (ch // SUBLANES, LANES), 0)

    def conv(r0, in_ctx):
        ext = xpad[pl.ds(r0, ext_rows), :]
        x = jnp.zeros((ch, LANES), F32) + cb
        for w in range(conv_w):
            xs = pltpu.roll(ext, ext_rows - (7 + w), 0)[0:ch]
            src = r0 + row + (w - 1)
            if in_ctx and w == 0:
                xs = jnp.where(src >= s, xs, 0.0)
            if not in_ctx and w >= 2:
                xs = jnp.where(src < s, xs, 0.0)
            x = x + xs * cw[w:w + 1]
        return x

    def chunk(r0, direction, carry, in_ctx):
        rows = pl.ds(r0, ch)
        if direction == 0:
            x = conv(r0, in_ctx)
            xc[rows, :] = x
        else:
            x = xc[rows, :]
        c0 = direction * 2 * LANES
        pre = _dot(_bf(x), wg_ref[0, :, c0:c0 + 2 * LANES]) + bg_ref[0, :, c0:c0 + 2 * LANES]
        r = jax.nn.sigmoid(pre[:, :LANES])
        i = jax.nn.sigmoid(pre[:, LANES:])
        log_a = -RG_C * r * sp[direction:direction + 1]
        a = jnp.exp(log_a)
        var = -_expm1(2.0 * log_a, a * a)
        bb = jnp.where(var > 0.0, var * lax.rsqrt(var), 0.0) * (i * x)
        return _scan_chunk(a, bb, carry, direction == 1, grow, ab_scr)

    carry = jnp.zeros((1, LANES), F32)
    for c in range(nctx):
        r0 = s + c * ch
        hh, carry = chunk(r0, 0, carry, True)
        hf[r0:r0 + ch, :] = hh

    def fwd(i, carry):
        for u in range(LRU_UNROLL):
            r0 = pl.multiple_of((i * LRU_UNROLL + u) * ch, ch)
            hh, carry = chunk(r0, 0, carry, False)
            hf[pl.ds(r0, ch), :] = hh
        return carry

    lax.fori_loop(0, nlat // LRU_UNROLL, fwd, carry)

    def emit(r0, hb):
        rows = pl.ds(r0, ch)
        y_ref[0, rows, :] = _bf((hf[rows, :] + hb) * jax.nn.gelu(gr_ref[0, rows, :]))

    carry = jnp.zeros((1, LANES), F32)
    for c in reversed(range(nctx)):
        r0 = s + c * ch
        hb, carry = chunk(r0, 1, carry, True)
        emit(r0, hb)

    def bwd(i, carry):
        for u in range(LRU_UNROLL):
            r0 = pl.multiple_of((nlat - 1 - i * LRU_UNROLL - u) * ch, ch)
            hb, carry = chunk(r0, 1, carry, False)
            emit(r0, hb)
        return carry

    lax.fori_loop(0, nlat // LRU_UNROLL, bwd, carry)


def _lru(xr, gr, p, s):
    b, l, d_rnn = xr.shape
    ng = d_rnn // LANES
    col = pl.BlockSpec((1, l, LANES), lambda i, j: (i, 0, j))
    return pl.pallas_call(
        functools.partial(_lru_kernel, s=s, ch=ROW_TILE),
        grid=(b, ng),
        in_specs=[col, col,
                  pl.BlockSpec((p["lru_cw"].shape[0], LANES), lambda i, j: (0, j)),
                  pl.BlockSpec((1, LANES), lambda i, j: (0, j)),
                  pl.BlockSpec((1, LANES, 4 * LANES), lambda i, j: (j, 0, 0)),
                  pl.BlockSpec((1, 1, 4 * LANES), lambda i, j: (j, 0, 0)),
                  pl.BlockSpec((1, 2, LANES), lambda i, j: (j, 0, 0))],
        out_specs=col,
        out_shape=jax.ShapeDtypeStruct((b, l, d_rnn), BF16),
        scratch_shapes=[pltpu.VMEM((l + 16, LANES), F32), pltpu.VMEM((l, LANES), F32),
                        pltpu.VMEM((l, LANES), F32), pltpu.VMEM((2, ROW_TILE, LANES), F32)],
        compiler_params=_params(("parallel", "parallel")),
        name="rglru",
    )(xr, gr, p["lru_cw"], p["lru_cb"], p["lru_wg"], p["lru_bg"], p["lru_lam"])


def _attn_kernel(*refs, diff, s, lam_init):
    if diff:
        q_ref, k_ref, vt_ref, lamv_ref, gout_ref, o_ref, s_buf, e_buf = refs
    else:
        q_ref, k_ref, vt_ref, o_ref, s_buf, e_buf = refs
    tq = q_ref.shape[1]
    nchunks, hdv, kc = vt_ref.shape[1:]
    nheads = q_ref.shape[2] // LANES
    dv = hdv // nheads
    per_block = LANES // dv
    grp = kc // 8
    subs = range(2 if diff else 1)
    if diff:
        lane = lax.broadcasted_iota(jnp.int32, (tq, LANES), 1)
        lv = lamv_ref[...]
        lam = (jnp.exp(jnp.sum(lv[0:1] * lv[1:2], axis=-1, keepdims=True))
               - jnp.exp(jnp.sum(lv[2:3] * lv[3:4], axis=-1, keepdims=True)) + lam_init)
        gout = gout_ref[...]

    def q_operands(h):
        qh = q_ref[0, :, h * LANES:(h + 1) * LANES]
        if not diff:
            return [qh]
        zq = jnp.zeros_like(qh)
        first = lane < LANES // 2
        return [jnp.where(first, qh, zq), jnp.where(first, zq, qh)]

    def fold(x, op):
        return op(x.reshape(grp, 8, tq), axis=0)

    def run(j_lo, j_hi):
        mrun = lsum = None
        pend = []
        for t in range(nheads + 2):
            ha, hb, hc = t, t - 1, t - 2
            do_a, do_b, do_c = ha < nheads, 0 <= hb < nheads, 0 <= hc < nheads
            if do_a:
                qs = q_operands(ha)
            if do_c:
                ls = [jnp.sum(lsum[i], axis=0, keepdims=True) for i in subs]
                rinv = 1.0 / ls[0]
                if diff:
                    cb = jnp.broadcast_to(_bf(lam * ls[0] / ls[1]), (kc, tq))
            if do_b:
                m8 = [jnp.broadcast_to(jnp.max(mrun[i], axis=0, keepdims=True), (8, tq)) for i in subs]
            mrun = [jnp.full((8, tq), -jnp.inf, F32) for _ in subs]
            lsum = [jnp.zeros((8, tq), F32) for _ in subs]
            oacc = jnp.zeros((dv, tq), F32)
            for j in range(j_lo, j_hi):
                if do_b:
                    sb = [s_buf[i, j] for i in subs]
                if do_c:
                    eb = [e_buf[i, j] for i in subs]
                if do_a:
                    kj = k_ref[0, j * kc:(j + 1) * kc, ha * LANES:(ha + 1) * LANES]
                    sc = [_dot_nt(kj, qs[i]) for i in subs]
                    for i in subs:
                        s_buf[i, j] = sc[i]
                    mrun = [jnp.maximum(mrun[i], fold(sc[i], jnp.max)) for i in subs]
                if do_b:
                    e = [jnp.exp2(sb[i].reshape(grp, 8, tq) - m8[i]) for i in subs]
                    lsum = [lsum[i] + jnp.sum(e[i], axis=0) for i in subs]
                    for i in subs:
                        e_buf[i, j] = _bf(e[i].reshape(kc, tq))
                if do_c:
                    w = eb[0] - eb[1] * cb if diff else eb[0]
                    oacc = oacc + _dot(vt_ref[0, j, hc * dv:(hc + 1) * dv, :], w)
            if do_c:
                pend.append(oacc * rinv)
                if len(pend) == per_block:
                    o = (pend[0] if per_block == 1 else jnp.concatenate(pend, axis=0)).T
                    if diff:
                        o = _rms(o, gout) * (1.0 - lam_init)
                    c0 = (hc // per_block) * LANES
                    o_ref[0, :, c0:c0 + LANES] = _bf(o)
                    pend = []

    t = pl.program_id(1)
    nlat = s // tq

    @pl.when(t < nlat)
    def _():
        run(0, nchunks)

    @pl.when(t >= nlat)
    def _():
        run(s // kc, nchunks)


def _attention(q, k, vt, s, nq_tiles, extra=(), lam_init=None):
    b, l, hl = q.shape
    tq = ROW_TILE
    nchunks, hdv, kc = vt.shape[1:]
    diff = lam_init is not None
    nsub = 2 if diff else 1
    whole = lambda a: pl.BlockSpec((1,) + a.shape[1:], lambda i, t: (i,) + (0,) * (a.ndim - 1),
                                   pipeline_mode=pl.Buffered(1))
    return pl.pallas_call(
        functools.partial(_attn_kernel, diff=diff, s=s, lam_init=lam_init),
        grid=(b, nq_tiles),
        in_specs=[pl.BlockSpec((1, tq, hl), lambda i, t: (i, t, 0)), whole(k), whole(vt)]
                 + [_const_spec(x.shape) for x in extra],
        out_specs=pl.BlockSpec((1, tq, hdv), lambda i, t: (i, t, 0)),
        out_shape=jax.ShapeDtypeStruct((b, l, hdv), BF16),
        scratch_shapes=[pltpu.VMEM((nsub, nchunks, kc, tq), F32), pltpu.VMEM((nsub, nchunks, kc, tq), BF16)],
        compiler_params=_params(("parallel", "arbitrary")),
        name="diff_attn" if diff else "mla_attn",
    )(q, k, vt, *extra)


def _diff_in_kernel(h_ref, mod_ref, g1_ref, win_ref, m_ref, invlen_ref, gainq_ref, gaink_ref,
                    cos_ref, sina_ref, sinb_ref, q_ref, k_ref, vt_ref, *, dh):
    tr = h_ref.shape[1]
    d = h_ref.shape[2]
    mod = mod_ref[0, 0]
    u = _rms(h_ref[0], g1_ref[...]) * (1.0 + mod[1:2]) + mod[0:1]
    z = _dot(_bf(u), win_ref[...])
    vt_ref[0, 0] = _bf(z[:, 2 * d:].T)
    cos, sina, sinb = cos_ref[...], sina_ref[...], sinb_ref[...]
    m, invlen = m_ref[...], invlen_ref[...]
    for src, gain_ref, dst in ((z[:, :d], gainq_ref, q_ref), (z[:, d:2 * d], gaink_ref, k_ref)):
        xn = _heads_norm(src, m, invlen, gain_ref[...], DIFF_HEADS)
        for h in range(DIFF_HEADS):
            dst[0, :, h * LANES:(h + 1) * LANES] = _bf(
                _rope(xn[h], cos, sina, sinb, dh // 4))


def _diff_in(h, modtab, g1, p, rope, nlat):
    b, l, d = h.shape
    tr = ROW_TILE
    row = pl.BlockSpec((1, tr, d), lambda i, t: (i, t, 0))
    tab = pl.BlockSpec((tr, LANES), lambda i, t: (t, 0))
    consts = [g1, p["w_in"], p["m"], p["invlen"], p["gainq"], p["gaink"]]
    return pl.pallas_call(
        functools.partial(_diff_in_kernel, dh=p["dh"]),
        grid=(b, l // tr),
        in_specs=[row, pl.BlockSpec((1, 1, 6, d), lambda i, t: (i, t // nlat, 0, 0))]
                 + [_const_spec(c.shape) for c in consts] + [tab, tab, tab],
        out_specs=[row, row, pl.BlockSpec((1, 1, d, tr), lambda i, t: (i, t, 0, 0))],
        out_shape=[jax.ShapeDtypeStruct((b, l, d), BF16), jax.ShapeDtypeStruct((b, l, d), BF16),
                   jax.ShapeDtypeStruct((b, l // tr, d, tr), BF16)],
        compiler_params=_params(("parallel", "parallel")),
        name="diff_in",
    )(h, modtab, *consts, *rope)


def _post_kernel(hm_ref, hp_ref, hn_ref, y1m_ref, y1p_ref, y1n_ref, y2m_ref, y2p_ref, y2n_ref,
                 mod_ref, g2_ref, wo1_ref, wo2_ref, wup_ref, cw_ref, cb_ref, wdn_ref, o_ref,
                 *, nlat, ntiles, ff_chunk):
    tr = hm_ref.shape[1]
    halo = hp_ref.shape[1]
    ext = tr + 2 * halo
    d_ff = wdn_ref.shape[0]
    mod = mod_ref[0, 0]
    cat = lambda p, m, n: jnp.concatenate([p[0], m[0], n[0]], axis=0)
    y = _dot(cat(y1p_ref, y1m_ref, y1n_ref), wo1_ref[...]) + _dot(cat(y2p_ref, y2m_ref, y2n_ref), wo2_ref[...])
    h1 = cat(hp_ref, hm_ref, hn_ref) + mod[2:3] * y
    u = _bf(_rms(h1, g2_ref[...]) * (1.0 + mod[4:5]) + mod[3:4])

    t = pl.program_id(1)
    row = lax.broadcasted_iota(jnp.int32, (tr, 1), 0)
    seg_start = jnp.logical_or(t == 0, t == nlat)
    seg_end = jnp.logical_or(t == nlat - 1, t == ntiles - 1)
    keep_prev = jnp.logical_not(jnp.logical_and(seg_start, row == 0))
    keep_next = jnp.logical_not(jnp.logical_and(seg_end, row == tr - 1))

    acc = jnp.zeros((tr, hm_ref.shape[2]), F32)
    for c0 in range(0, d_ff, ff_chunk):
        a = _dot(u[halo:halo + tr], wup_ref[:, c0:c0 + ff_chunk])
        g = _dot(u, wup_ref[:, d_ff + c0:d_ff + c0 + ff_chunk])
        cw = cw_ref[:, c0:c0 + ff_chunk]
        g_prev = jnp.where(keep_prev, pltpu.roll(g, 1, 0)[halo:halo + tr], 0.0)
        g_next = jnp.where(keep_next, pltpu.roll(g, ext - 1, 0)[halo:halo + tr], 0.0)
        gc = (cb_ref[:, c0:c0 + ff_chunk] + g_prev * cw[0:1] + g[halo:halo + tr] * cw[1:2]
              + g_next * cw[2:3])
        acc = acc + _dot(_bf(jax.nn.gelu(gc) * a), wdn_ref[c0:c0 + ff_chunk, :])
    o_ref[0] = h1[halo:halo + tr] + mod[5:6] * acc


def _post(h, y1, y2, y2_col, modtab, g2, wo1, wo2, fp, nlat, out_tiles):
    b, l, d = h.shape
    tr, halo = ROW_TILE, BF16_ROWS
    ntiles = l // tr
    per = tr // halo
    last = out_tiles * per - 1
    yw = wo1.shape[0]
    main = lambda w, c: pl.BlockSpec((1, tr, w), lambda i, t: (i, t, c))
    prev = lambda w, c: pl.BlockSpec((1, halo, w), lambda i, t: (i, jnp.maximum(t * per - 1, 0), c))
    nxt = lambda w, c: pl.BlockSpec((1, halo, w), lambda i, t: (i, jnp.minimum((t + 1) * per, last), c))
    consts = [g2, wo1, wo2, fp["w_up"], fp["cw"], fp["cb"], fp["w_down"]]
    d_ff = fp["w_down"].shape[0]
    kern = functools.partial(_post_kernel, nlat=nlat, ntiles=ntiles, ff_chunk=d_ff // 2)
    return pl.pallas_call(
        kern,
        grid=(b, out_tiles),
        in_specs=[main(d, 0), prev(d, 0), nxt(d, 0),
                  main(yw, 0), prev(yw, 0), nxt(yw, 0),
                  main(yw, y2_col), prev(yw, y2_col), nxt(yw, y2_col),
                  pl.BlockSpec((1, 1, 6, d), lambda i, t: (i, t // nlat, 0, 0))]
                 + [_const_spec(c.shape) for c in consts],
        out_specs=main(d, 0),
        out_shape=jax.ShapeDtypeStruct((b, out_tiles * tr, d), F32),
        compiler_params=_params(("parallel", "parallel")),
        name="post",
    )(h, h, h, y1, y1, y1, y2, y2, y2, modtab, *consts)


def _rope_tables(s, l, da, blocks):
    half = da // 2
    rows = s // GRID_W
    row = jnp.repeat(jnp.arange(rows, dtype=F32), GRID_W)
    col = jnp.tile(jnp.arange(GRID_W, dtype=F32), rows)
    inv = ROPE_THETA ** (-jnp.arange(0, da, 2, dtype=F32) / da)
    ar, ac = row[:, None] * inv, col[:, None] * inv
    cr, sr, cc, sc = jnp.cos(ar), jnp.sin(ar), jnp.cos(ac), jnp.sin(ac)
    z = jnp.zeros((s, half), F32)
    cos_p, sina_p, sinb_p = [], [], []
    for blk in blocks:
        if blk[0] == "id":
            cos_p.append(jnp.ones((s, blk[1]), F32))
            sina_p.append(jnp.zeros((s, blk[1]), F32))
            sinb_p.append(jnp.zeros((s, blk[1]), F32))
        else:
            cos_p += [cr, cr, cc, cc]
            sina_p += [-sr, z, -sc, z]
            sinb_p += [z, sr, z, sc]
    pad = lambda x, v: jnp.concatenate([x, jnp.full((l - s, LANES), v, F32)], axis=0)
    return (pad(jnp.concatenate(cos_p, axis=1), 1.0), pad(jnp.concatenate(sina_p, axis=1), 0.0),
            pad(jnp.concatenate(sinb_p, axis=1), 0.0))


def _seg_matrix(seg_ids):
    ids = np.asarray(seg_ids)
    return jnp.asarray((ids[:, None] == ids[None, :]).astype(np.float32), BF16)


def _lanes(*parts):
    return jnp.concatenate([jnp.asarray(p, F32).reshape(-1) for p in parts]).reshape(1, -1)


def _even_params(e, d, w_in_ab, lru_conv_w, lru_conv_b, lru_gate_w, lru_gate_b, lru_lambda, mla_g_q,
                 mla_g_kv, mla_w_uq, mla_w_ukv, gqn, gkn, gqr, gkr, w_out_ab):
    d_rnn = lru_conv_w.shape[2]
    q_rank, kv_rank = mla_g_q.shape[1], mla_g_kv.shape[1]
    nope, rope, hh = MLA_NOPE, MLA_ROPE, MLA_HEADS
    dv = mla_w_ukv.shape[2] // hh - nope
    w_in = w_in_ab[e]
    base = 2 * d_rnn + q_rank + kv_rank
    w_in_pad = jnp.concatenate([w_in[:, :base], jnp.zeros((d, nope), F32), w_in[:, base:],
                                jnp.zeros((d, LANES - nope - rope), F32)], axis=1)
    w_uq = mla_w_uq[e].reshape(q_rank, hh, nope + rope)
    w_uq = jnp.pad(w_uq, ((0, 0), (0, 0), (0, LANES - nope - rope))).reshape(q_rank, hh * LANES)
    w_ukv = mla_w_ukv[e].reshape(kv_rank, hh, nope + dv)
    w_uk = jnp.pad(w_ukv[:, :, :nope], ((0, 0), (0, 0), (0, LANES - nope))).reshape(kv_rank, hh * LANES)
    w_uv = w_ukv[:, :, nope:].reshape(kv_rank, hh * dv)
    scale = (nope + rope) ** -0.5 * LOG2E
    rest = LANES - nope - rope
    gw = lru_gate_w[e]
    bs = gw.shape[-1]
    per = LANES // bs
    ng = d_rnn // LANES
    gwg = gw.reshape(2, 2, ng, per, bs, bs)
    eye = jnp.eye(per, dtype=F32)
    wg = jnp.einsum("dgjpcn,pq->jpcdgqn", gwg, eye).reshape(ng, LANES, 4 * LANES)
    bg = lru_gate_b[e].reshape(2, 2, ng, LANES).transpose(2, 0, 1, 3).reshape(ng, 1, 4 * LANES)
    lam = lru_lambda[e].reshape(2, ng, LANES).transpose(1, 0, 2)
    return {
        "d_rnn": d_rnn,
        "w_in": _bf(w_in_pad), "g_q": mla_g_q[e][None], "g_kv": mla_g_kv[e][None],
        "w_uq": _bf(w_uq), "w_uk": _bf(w_uk), "w_uv": _bf(w_uv),
        "m": _seg_matrix(([0] * nope + [1] * rope + [2] * rest) + ([3] * nope + [4] * rope + [5] * rest)),
        "invlen": _lanes(np.full(nope, 1.0 / nope), np.full(LANES - nope, 1.0 / rope)),
        "gainq": _lanes(gqn[e] * scale, gqr[e] * scale, np.zeros(rest)),
        "gaink": _lanes(gkn[e], np.zeros(LANES - nope)),
        "gkr": _lanes(np.zeros(nope), gkr[e], np.zeros(rest)),
        "lru_cw": lru_conv_w[e], "lru_cb": lru_conv_b[e][None],
        "lru_wg": _bf(wg), "lru_bg": bg, "lru_lam": lam,
        "wo1": _bf(w_out_ab[e][:d_rnn]), "wo2": _bf(w_out_ab[e][d_rnn:]),
    }


def _odd_params(o, d, w_in_c, diff_gq, diff_gk, diff_lam, diff_g_out, w_out_c):
    dh = diff_gq.shape[1]
    scale = dh ** -0.5 * LOG2E
    return {
        "dh": dh,
        "w_in": _bf(w_in_c[o]),
        "m": _seg_matrix([0] * dh + [1] * dh + [2] * dh + [3] * dh),
        "invlen": _lanes(np.full(LANES, 1.0 / dh)),
        "gainq": _lanes(diff_gq[o] * scale, diff_gq[o] * scale),
        "gaink": _lanes(diff_gk[o], diff_gk[o]),
        "lam_vecs": diff_lam[o], "g_out": diff_g_out[o][None],
        "wo1": _bf(w_out_c[o][:d // 2]), "wo2": _bf(w_out_c[o][d // 2:]),
    }


def kernel(x, c, ctx, c_ctx, w_mod, b_mod, g_norm1, g_norm2, w_in_ab, lru_conv_w, lru_conv_b, lru_gate_w, lru_gate_b, lru_lambda, mla_g_q, mla_g_kv, mla_w_uq, mla_w_ukv, mla_gq_nope, mla_gk_nope, mla_gq_rope, mla_gk_rope, w_out_ab, w_in_c, diff_gq, diff_gk, diff_lam, diff_g_out, w_out_c, ffn_w_up, ffn_conv_w, ffn_conv_b, ffn_w_down):
    b, s, d = x.shape
    n_ctx = ctx.shape[1]
    l = s + n_ctx
    depth = w_mod.shape[0]
    tr = ROW_TILE
    assert s % tr == 0 and n_ctx == tr and s % GRID_W == 0 and d == DIFF_HEADS * LANES
    nlat, ntiles = s // tr, l // tr

    cc = jnp.concatenate([c, c_ctx[None], jnp.zeros((BF16_ROWS - (b + 1) % BF16_ROWS, d), F32)], axis=0)
    mod = _modulation(cc, w_mod, b_mod)
    mod_lat = mod[:, :b].reshape(depth, b, 1, 6, d)
    mod_ctx = jnp.broadcast_to(mod[:, b].reshape(depth, 1, 1, 6, d), (depth, b, 1, 6, d))
    modtab = jnp.concatenate([mod_lat, mod_ctx], axis=2)

    rope_mla = _rope_tables(s, l, MLA_ROPE // 2,
                            [("id", MLA_NOPE), ("rot",), ("id", LANES - MLA_NOPE - MLA_ROPE)])
    dh = diff_gq.shape[1]
    rope_diff = _rope_tables(s, l, dh // 2, [("rot",), ("rot",)])

    h = jnp.concatenate([x, ctx], axis=1)
    for li in range(depth):
        last = li == depth - 1
        out_tiles = nlat if last else ntiles
        fp = {"w_up": _bf(ffn_w_up[li]), "cw": ffn_conv_w[li], "cb": ffn_conv_b[li][None],
              "w_down": _bf(ffn_w_down[li])}
        g1, g2 = g_norm1[li][None], g_norm2[li][None]
        if li % 2 == 0:
            p = _even_params(li // 2, d, w_in_ab, lru_conv_w, lru_conv_b, lru_gate_w, lru_gate_b,
                             lru_lambda, mla_g_q, mla_g_kv, mla_w_uq, mla_w_ukv, mla_gq_nope,
                             mla_gk_nope, mla_gq_rope, mla_gk_rope, w_out_ab)
            xr, gr, q, k, vt = _even_in(h, modtab[li], g1, p, rope_mla, nlat)
            y1 = _lru(xr, gr, p, s)
            y2 = _attention(q, k, vt, s, out_tiles)
            h = _post(h, y1, y2, 0, modtab[li], g2, p["wo1"], p["wo2"], fp, nlat, out_tiles)
        else:
            p = _odd_params(li // 2, d, w_in_c, diff_gq, diff_gk, diff_lam, diff_g_out, w_out_c)
            lam_init = 0.8 - 0.6 * math.exp(-0.3 * li)
            q, k, vt = _diff_in(h, modtab[li], g1, p, rope_diff, nlat)
            y = _attention(q, k, vt, s, out_tiles, (p["lam_vecs"], p["g_out"]), lam_init)
            h = _post(h, y, y, 1, modtab[li], g2, p["wo1"], p["wo2"], fp, nlat, out_tiles)
    return h
```

```python
import functools
import math

import numpy as np
import jax
import jax.numpy as jnp
from jax import lax
from jax.experimental import pallas as pl
from jax.experimental.pallas import tpu as pltpu

F32 = jnp.float32
BF16 = jnp.bfloat16

EPS = 1e-6
ROPE_THETA = 10000.0
GRID_W = 64
RG_C = 8.0
LRU_BLOCKS = 8
MLA_HEADS = 8
MLA_NOPE = 64
MLA_ROPE = 32
DIFF_HEADS = 8

LANES = 128
SUBLANES = 8
MXU_TILE = 256
BF16_ROWS = 16
ROW_TILE = 256
LRU_UNROLL = 2
POST_TILE = 512
FF_CHUNKS = 4
LOG2E = 1.4426950408889634
VMEM_LIMIT = 52 * 1024 * 1024


def _bf(x):
    return x.astype(BF16)


def _dot(a, b):
    return jnp.dot(a, b, preferred_element_type=F32)


def _dot_nt(a, b):
    return lax.dot_general(a, b, (((1,), (1,)), ((), ())), preferred_element_type=F32)


def _rms(x, g):
    ms = jnp.mean(x * x, axis=-1, keepdims=True)
    return x * lax.rsqrt(ms + EPS) * g


def _expm1(x, u):
    small = jnp.where(u == 1.0, x, (u - 1.0) * x / jnp.log(u))
    return jnp.where(jnp.abs(x) > 1.0, u - 1.0, small)


def _log1p(y):
    w = 1.0 + y
    return jnp.where(w == 1.0, y, jnp.log(w) * y / (w - 1.0))


def _seg_sumsq(x, m):
    x2 = x * x
    hi = _bf(x2)
    lo = _bf(x2 - hi.astype(F32))
    return _dot(hi, m) + _dot(lo, m)


def _heads_norm(x, m, invlen, gain, nheads):
    r = x.shape[0]
    pw = m.shape[0]
    per = pw // LANES
    stacked = jnp.concatenate([x[:, p * pw:(p + 1) * pw] for p in range(nheads // per)], axis=0)
    ss = _seg_sumsq(stacked, m)
    y = stacked * lax.rsqrt(ss * jnp.tile(invlen, (1, per)) + EPS) * jnp.tile(gain, (1, per))
    return [y[(h // per) * r:(h // per + 1) * r, (h % per) * LANES:(h % per + 1) * LANES]
            for h in range(nheads)]


def _rope(x, cos, sina, sinb, shift):
    return x * cos + pltpu.roll(x, LANES - shift, 1) * sina + pltpu.roll(x, shift, 1) * sinb


def _const_spec(shape):
    n = len(shape)
    return pl.BlockSpec(shape, lambda *_: (0,) * n, pipeline_mode=pl.Buffered(1))


def _params(sem):
    return pltpu.CompilerParams(dimension_semantics=sem, vmem_limit_bytes=VMEM_LIMIT)


def _mod_kernel(cc_ref, w_ref, b_ref, o_ref):
    x = cc_ref[...]
    sx = x * jax.nn.sigmoid(x)
    o_ref[0] = _dot(_bf(sx), _bf(w_ref[0])) + b_ref[0]


def _modulation(cc, w_mod, b_mod):
    depth, d, d6 = w_mod.shape
    rows = cc.shape[0]
    return pl.pallas_call(
        _mod_kernel,
        grid=(depth, d6 // d),
        in_specs=[pl.BlockSpec((rows, d), lambda l, j: (0, 0)),
                  pl.BlockSpec((1, d, d), lambda l, j: (l, 0, j)),
                  pl.BlockSpec((1, 1, d), lambda l, j: (l, 0, j))],
        out_specs=pl.BlockSpec((1, rows, d), lambda l, j: (l, 0, j)),
        out_shape=jax.ShapeDtypeStruct((depth, rows, d6), F32),
        compiler_params=_params(("arbitrary", "arbitrary")),
        name="modulation",
    )(cc, w_mod, b_mod.reshape(depth, 1, d6))


def _even_in_kernel(h_ref, mod_ref, g1_ref, win_ref, gq_ref, gkv_ref, wuq_ref, wuk_ref, wuv_ref,
                    m_ref, invlen_ref, gainq_ref, gaink_ref, gkr_ref, cos_ref, sina_ref, sinb_ref,
                    xr_ref, gr_ref, q_ref, k_ref, vt_ref, *, d_rnn, q_rank, kv_rank):
    tr = h_ref.shape[1]
    mod = mod_ref[0, 0]
    u = _rms(h_ref[0], g1_ref[...]) * (1.0 + mod[1:2]) + mod[0:1]
    z = _dot(_bf(u), win_ref[...])
    o = 2 * d_rnn
    xr_ref[0] = z[:, :d_rnn]
    gr_ref[0] = z[:, d_rnn:o]
    ql = z[:, o:o + q_rank]
    kvl = z[:, o + q_rank:o + q_rank + kv_rank]
    kr = z[:, o + q_rank + kv_rank:]

    cos, sina, sinb = cos_ref[...], sina_ref[...], sinb_ref[...]
    m, invlen = m_ref[...], invlen_ref[...]

    q = _dot(_bf(_rms(ql, gq_ref[...])), wuq_ref[...])
    qn = _heads_norm(q, m, invlen, gainq_ref[...], MLA_HEADS)
    for h in range(MLA_HEADS):
        q_ref[0, :, h * LANES:(h + 1) * LANES] = _bf(
            _rope(qn[h], cos, sina, sinb, MLA_ROPE // 4))

    kvn = _bf(_rms(kvl, gkv_ref[...]))
    vt_ref[0, 0] = _bf(_dot(kvn, wuv_ref[...]).T)
    kn = _heads_norm(_dot(kvn, wuk_ref[...]), m, invlen, gaink_ref[...], MLA_HEADS)
    ss = jnp.sum(kr * kr, axis=-1, keepdims=True) * (1.0 / MLA_ROPE)
    k_rope = _rope(kr * lax.rsqrt(ss + EPS) * gkr_ref[...], cos, sina, sinb, MLA_ROPE // 4)
    for h in range(MLA_HEADS):
        k_ref[0, :, h * LANES:(h + 1) * LANES] = _bf(kn[h] + k_rope)


def _even_in(h, modtab, g1, p, rope, nlat):
    b, l, d = h.shape
    tr = ROW_TILE
    d_rnn = p["d_rnn"]
    hl = MLA_HEADS * LANES
    hdv = p["w_uv"].shape[1]
    row = lambda w: pl.BlockSpec((1, tr, w), lambda i, t: (i, t, 0))
    tab = pl.BlockSpec((tr, LANES), lambda i, t: (t, 0))
    consts = [g1, p["w_in"], p["g_q"], p["g_kv"], p["w_uq"], p["w_uk"], p["w_uv"],
              p["m"], p["invlen"], p["gainq"], p["gaink"], p["gkr"]]
    kern = functools.partial(_even_in_kernel, d_rnn=d_rnn, q_rank=p["g_q"].shape[1],
                             kv_rank=p["g_kv"].shape[1])
    return pl.pallas_call(
        kern,
        grid=(b, l // tr),
        in_specs=[row(d), pl.BlockSpec((1, 1, 6, d), lambda i, t: (i, t // nlat, 0, 0))]
                 + [_const_spec(c.shape) for c in consts] + [tab, tab, tab],
        out_specs=[row(d_rnn), row(d_rnn), row(hl), row(hl),
                   pl.BlockSpec((1, 1, hdv, tr), lambda i, t: (i, t, 0, 0))],
        out_shape=[jax.ShapeDtypeStruct((b, l, d_rnn), F32), jax.ShapeDtypeStruct((b, l, d_rnn), F32),
                   jax.ShapeDtypeStruct((b, l, hl), BF16), jax.ShapeDtypeStruct((b, l, hl), BF16),
                   jax.ShapeDtypeStruct((b, l // tr, hdv, tr), BF16)],
        compiler_params=_params(("parallel", "parallel")),
        name="even_in",
    )(h, modtab, *consts, *rope)


def _doubling_scan(a, b, pos, reverse):
    axis = a.ndim - 2
    n = a.shape[axis]
    d = 1
    while d < n:
        if not reverse:
            keep = pos >= d
            a_sh = jnp.where(keep, pltpu.roll(a, d, axis), 1.0)
            b_sh = jnp.where(keep, pltpu.roll(b, d, axis), 0.0)
        else:
            keep = pos < n - d
            a_sh = jnp.where(keep, pltpu.roll(a, n - d, axis), 1.0)
            b_sh = jnp.where(keep, pltpu.roll(b, n - d, axis), 0.0)
        b = a * b_sh + b
        a = a * a_sh
        d *= 2
    return a, b


def _scan_chunk(a, b, carry, reverse, grow, ab_scr):
    ch = a.shape[0]
    ngroups = ch // SUBLANES
    grouped = (ngroups, SUBLANES, LANES)
    a, b = _doubling_scan(a.reshape(grouped), b.reshape(grouped),
                          lax.broadcasted_iota(jnp.int32, grouped, 1), reverse)
    a, b = a.reshape(ch, LANES), b.reshape(ch, LANES)
    ab_scr[0] = a
    ab_scr[1] = b
    last = 0 if reverse else SUBLANES - 1
    ga = ab_scr[0, pl.ds(last, ngroups, stride=SUBLANES), :]
    gb = ab_scr[1, pl.ds(last, ngroups, stride=SUBLANES), :]
    ga, gb = _doubling_scan(ga, gb, grow, reverse)
    leaving = gb + ga * carry
    if not reverse:
        entering = jnp.where(grow >= 1, pltpu.roll(leaving, 1, 0), carry)
        carry = leaving[ngroups - 1:ngroups]
    else:
        entering = jnp.where(grow < ngroups - 1, pltpu.roll(leaving, ngroups - 1, 0), carry)
        carry = leaving[0:1]
    out = [b[g * SUBLANES:(g + 1) * SUBLANES] + a[g * SUBLANES:(g + 1) * SUBLANES] * entering[g:g + 1]
           for g in range(ngroups)]
    return jnp.concatenate(out, axis=0), carry


def _lru_kernel(xr_ref, gr_ref, cw_ref, cb_ref, wg_ref, bg_ref, lam_ref, y_ref, xpad, xc, hf, ab_scr,
                *, s, ch):
    l = xr_ref.shape[1]
    nlat, nctx = s // ch, (l - s) // ch
    conv_w = cw_ref.shape[0]
    ext_rows = ch + 16
    xpad[0:8, :] = jnp.zeros((8, LANES), F32)
    xpad[8:8 + l, :] = xr_ref[0]
    xpad[8 + l:16 + l, :] = jnp.zeros((8, LANES), F32)
    cw, cb = cw_ref[...], cb_ref[...]
    nl = -lam_ref[0]
    sp = jnp.maximum(nl, 0.0) + _log1p(jnp.exp(-jnp.abs(nl)))
    row = lax.broadcasted_iota(jnp.int32, (ch, LANES), 0)
    grow = lax.broadcasted_iota(jnp.int32, (ch // SUBLANES, LANES), 0)

    def conv(r0, in_ctx):
        ext = xpad[pl.ds(r0, ext_rows), :]
        x = jnp.zeros((ch, LANES), F32) + cb
        for w in range(conv_w):
            xs = pltpu.roll(ext, ext_rows - (7 + w), 0)[0:ch]
            src = r0 + row + (w - 1)
            if in_ctx and w == 0:
                xs = jnp.where(src >= s, xs, 0.0)
            if not in_ctx and w >= 2:
                xs = jnp.where(src < s, xs, 0.0)
            x = x + xs * cw[w:w + 1]
        return x

    def chunk(r0, direction, carry, in_ctx):
        rows = pl.ds(r0, ch)
        if direction == 0:
            x = conv(r0, in_ctx)
            xc[rows, :] = x
        else:
            x = xc[rows, :]
        c0 = direction * 2 * LANES
        pre = _dot(_bf(x), wg_ref[0, :, c0:c0 + 2 * LANES]) + bg_ref[0, :, c0:c0 + 2 * LANES]
        r = jax.nn.sigmoid(pre[:, :LANES])
        i = jax.nn.sigmoid(pre[:, LANES:])
        log_a = -RG_C * r * sp[direction:direction + 1]
        a = jnp.exp(log_a)
        var = -_expm1(2.0 * log_a, a * a)
        bb = jnp.where(var > 0.0, var * lax.rsqrt(var), 0.0) * (i * x)
        return _scan_chunk(a, bb, carry, direction == 1, grow, ab_scr)

    carry = jnp.zeros((1, LANES), F32)
    for c in range(nctx):
        r0 = s + c * ch
        hh, carry = chunk(r0, 0, carry, True)
        hf[r0:r0 + ch, :] = hh

    def fwd(i, carry):
        for u in range(LRU_UNROLL):
            r0 = pl.multiple_of((i * LRU_UNROLL + u) * ch, ch)
            hh, carry = chunk(r0, 0, carry, False)
            hf[pl.ds(r0, ch), :] = hh
        return carry

    lax.fori_loop(0, nlat // LRU_UNROLL, fwd, carry)

    def emit(r0, hb):
        rows = pl.ds(r0, ch)
        y_ref[0, rows, :] = _bf((hf[rows, :] + hb) * jax.nn.gelu(gr_ref[0, rows, :]))

    carry = jnp.zeros((1, LANES), F32)
    for c in reversed(range(nctx)):
        r0 = s + c * ch
        hb, carry = chunk(r0, 1, carry, True)
        emit(r0, hb)

    def bwd(i, carry):
        for u in range(LRU_UNROLL):
            r0 = pl.multiple_of((nlat - 1 - i * LRU_UNROLL - u) * ch, ch)
            hb, carry = chunk(r0, 1, carry, False)
            emit(r0, hb)
        return carry

    lax.fori_loop(0, nlat // LRU_UNROLL, bwd, carry)


def _lru(xr, gr, p, s):
    b, l, d_rnn = xr.shape
    ng = d_rnn // LANES
    col = pl.BlockSpec((1, l, LANES), lambda i, j: (i, 0, j))
    return pl.pallas_call(
        functools.partial(_lru_kernel, s=s, ch=ROW_TILE),
        grid=(b, ng),
        in_specs=[col, col,
                  pl.BlockSpec((p["lru_cw"].shape[0], LANES), lambda i, j: (0, j)),
                  pl.BlockSpec((1, LANES), lambda i, j: (0, j)),
                  pl.BlockSpec((1, LANES, 4 * LANES), lambda i, j: (j, 0, 0)),
                  pl.BlockSpec((1, 1, 4 * LANES), lambda i, j: (j, 0, 0)),
                  pl.BlockSpec((1, 2, LANES), lambda i, j: (j, 0, 0))],
        out_specs=col,
        out_shape=jax.ShapeDtypeStruct((b, l, d_rnn), BF16),
        scratch_shapes=[pltpu.VMEM((l + 16, LANES), F32), pltpu.VMEM((l, LANES), F32),
                        pltpu.VMEM((l, LANES), F32), pltpu.VMEM((2, ROW_TILE, LANES), F32)],
        compiler_params=_params(("parallel", "parallel")),
        name="rglru",
    )(xr, gr, p["lru_cw"], p["lru_cb"], p["lru_wg"], p["lru_bg"], p["lru_lam"])


def _attn_kernel(*refs, diff, s, lam_init):
    if diff:
        q_ref, k_ref, vt_ref, lamv_ref, gout_ref, o_ref, s_buf, e_buf = refs
    else:
        q_ref, k_ref, vt_ref, o_ref, s_buf, e_buf = refs
    tq = q_ref.shape[1]
    nchunks, hdv, kc = vt_ref.shape[1:]
    nheads = q_ref.shape[2] // LANES
    dv = hdv // nheads
    per_block = LANES // dv
    grp = kc // 8
    subs = range(2 if diff else 1)
    if diff:
        lane = lax.broadcasted_iota(jnp.int32, (tq, LANES), 1)
        lv = lamv_ref[...]
        lam = (jnp.exp(jnp.sum(lv[0:1] * lv[1:2], axis=-1, keepdims=True))
               - jnp.exp(jnp.sum(lv[2:3] * lv[3:4], axis=-1, keepdims=True)) + lam_init)
        gout = gout_ref[...]

    def q_operands(h):
        qh = q_ref[0, :, h * LANES:(h + 1) * LANES]
        if not diff:
            return [qh]
        zq = jnp.zeros_like(qh)
        first = lane < LANES // 2
        return [jnp.where(first, qh, zq), jnp.where(first, zq, qh)]

    def fold(x, op):
        return op(x.reshape(grp, 8, tq), axis=0)

    def run(j_lo, j_hi):
        mrun = lsum = None
        pend = []
        for t in range(nheads + 2):
            ha, hb, hc = t, t - 1, t - 2
            do_a, do_b, do_c = ha < nheads, 0 <= hb < nheads, 0 <= hc < nheads
            if do_a:
                qs = q_operands(ha)
            if do_c:
                ls = [jnp.sum(lsum[i], axis=0, keepdims=True) for i in subs]
                rinv = 1.0 / ls[0]
                if diff:
                    cb = jnp.broadcast_to(_bf(lam * ls[0] / ls[1]), (kc, tq))
            if do_b:
                m8 = [jnp.broadcast_to(jnp.max(mrun[i], axis=0, keepdims=True), (8, tq)) for i in subs]
            mrun = [jnp.full((8, tq), -jnp.inf, F32) for _ in subs]
            lsum = [jnp.zeros((8, tq), F32) for _ in subs]
            oacc = jnp.zeros((dv, tq), F32)
            for j in range(j_lo, j_hi):
                if do_b:
                    sb = [s_buf[i, j] for i in subs]
                if do_c:
                    eb = [e_buf[i, j] for i in subs]
                if do_a:
                    kj = k_ref[0, j * kc:(j + 1) * kc, ha * LANES:(ha + 1) * LANES]
                    sc = [_dot_nt(kj, qs[i]) for i in subs]
                    for i in subs:
                        s_buf[i, j] = sc[i]
                    mrun = [jnp.maximum(mrun[i], fold(sc[i], jnp.max)) for i in subs]
                if do_b:
                    e = [jnp.exp2(sb[i].reshape(grp, 8, tq) - m8[i]) for i in subs]
                    lsum = [lsum[i] + jnp.sum(e[i], axis=0) for i in subs]
                    for i in subs:
                        e_buf[i, j] = _bf(e[i].reshape(kc, tq))
                if do_c:
                    w = eb[0] - eb[1] * cb if diff else eb[0]
                    oacc = oacc + _dot(vt_ref[0, j, hc * dv:(hc + 1) * dv, :], w)
            if do_c:
                pend.append(oacc * rinv)
                if len(pend) == per_block:
                    o = (pend[0] if per_block == 1 else jnp.concatenate(pend, axis=0)).T
                    if diff:
                        o = _rms(o, gout) * (1.0 - lam_init)
                    c0 = (hc // per_block) * LANES
                    o_ref[0, :, c0:c0 + LANES] = _bf(o)
                    pend = []

    t = pl.program_id(1)
    nlat = s // tq

    @pl.when(t < nlat)
    def _():
        run(0, nchunks)

    @pl.when(t >= nlat)
    def _():
        run(s // kc, nchunks)


def _attention(q, k, vt, s, nq_tiles, extra=(), lam_init=None):
    b, l, hl = q.shape
    tq = ROW_TILE
    nchunks, hdv, kc = vt.shape[1:]
    diff = lam_init is not None
    nsub = 2 if diff else 1
    whole = lambda a: pl.BlockSpec((1,) + a.shape[1:], lambda i, t: (i,) + (0,) * (a.ndim - 1),
                                   pipeline_mode=pl.Buffered(1))
    return pl.pallas_call(
        functools.partial(_attn_kernel, diff=diff, s=s, lam_init=lam_init),
        grid=(b, nq_tiles),
        in_specs=[pl.BlockSpec((1, tq, hl), lambda i, t: (i, t, 0)), whole(k), whole(vt)]
                 + [_const_spec(x.shape) for x in extra],
        out_specs=pl.BlockSpec((1, tq, hdv), lambda i, t: (i, t, 0)),
        out_shape=jax.ShapeDtypeStruct((b, l, hdv), BF16),
        scratch_shapes=[pltpu.VMEM((nsub, nchunks, kc, tq), F32), pltpu.VMEM((nsub, nchunks, kc, tq), BF16)],
        compiler_params=_params(("parallel", "arbitrary")),
        name="diff_attn" if diff else "mla_attn",
    )(q, k, vt, *extra)


def _diff_in_kernel(h_ref, mod_ref, g1_ref, win_ref, m_ref, invlen_ref, gainq_ref, gaink_ref,
                    cos_ref, sina_ref, sinb_ref, q_ref, k_ref, vt_ref, *, dh):
    tr = h_ref.shape[1]
    d = h_ref.shape[2]
    mod = mod_ref[0, 0]
    u = _rms(h_ref[0], g1_ref[...]) * (1.0 + mod[1:2]) + mod[0:1]
    z = _dot(_bf(u), win_ref[...])
    vt_ref[0, 0] = _bf(z[:, 2 * d:].T)
    cos, sina, sinb = cos_ref[...], sina_ref[...], sinb_ref[...]
    m, invlen = m_ref[...], invlen_ref[...]
    for src, gain_ref, dst in ((z[:, :d], gainq_ref, q_ref), (z[:, d:2 * d], gaink_ref, k_ref)):
        xn = _heads_norm(src, m, invlen, gain_ref[...], DIFF_HEADS)
        for h in range(DIFF_HEADS):
            dst[0, :, h * LANES:(h + 1) * LANES] = _bf(
                _rope(xn[h], cos, sina, sinb, dh // 4))


def _diff_in(h, modtab, g1, p, rope, nlat):
    b, l, d = h.shape
    tr = ROW_TILE
    row = pl.BlockSpec((1, tr, d), lambda i, t: (i, t, 0))
    tab = pl.BlockSpec((tr, LANES), lambda i, t: (t, 0))
    consts = [g1, p["w_in"], p["m"], p["invlen"], p["gainq"], p["gaink"]]
    return pl.pallas_call(
        functools.partial(_diff_in_kernel, dh=p["dh"]),
        grid=(b, l // tr),
        in_specs=[row, pl.BlockSpec((1, 1, 6, d), lambda i, t: (i, t // nlat, 0, 0))]
                 + [_const_spec(c.shape) for c in consts] + [tab, tab, tab],
        out_specs=[row, row, pl.BlockSpec((1, 1, d, tr), lambda i, t: (i, t, 0, 0))],
        out_shape=[jax.ShapeDtypeStruct((b, l, d), BF16), jax.ShapeDtypeStruct((b, l, d), BF16),
                   jax.ShapeDtypeStruct((b, l // tr, d, tr), BF16)],
        compiler_params=_params(("parallel", "parallel")),
        name="diff_in",
    )(h, modtab, *consts, *rope)


def _post_kernel(hm_ref, hp_ref, hn_ref, y1m_ref, y1p_ref, y1n_ref, y2m_ref, y2p_ref, y2n_ref,
                 mod_ref, g2_ref, wo1_ref, wo2_ref, wup_ref, cw_ref, cb_ref, wdn_ref, *rest, ff_chunks):
    o_ref = rest[-1]
    tr = hm_ref.shape[1]
    halo = hp_ref.shape[1]
    ext = tr + 2 * halo
    d_ff = wdn_ref.shape[0]
    mod = mod_ref[0, 0]
    cat = lambda p, m, n: jnp.concatenate([p[0], m[0], n[0]], axis=0)
    y = _dot(cat(y1p_ref, y1m_ref, y1n_ref), wo1_ref[...]) + _dot(cat(y2p_ref, y2m_ref, y2n_ref), wo2_ref[...])
    h1 = cat(hp_ref, hm_ref, hn_ref) + mod[2:3] * y
    u = _bf(_rms(h1, g2_ref[...]) * (1.0 + mod[4:5]) + mod[3:4])

    t = pl.program_id(1)
    row = lax.broadcasted_iota(jnp.int32, (tr, 1), 0)
    keep_prev = jnp.logical_not(jnp.logical_and(t == 0, row == 0))
    keep_next = jnp.logical_not(jnp.logical_and(t == pl.num_programs(1) - 1, row == tr - 1))

    acc = jnp.zeros((tr, hm_ref.shape[2]), F32)
    for c0, c1 in zip(ff_chunks[:-1], ff_chunks[1:]):
        a = _dot(u[halo:halo + tr], wup_ref[:, c0:c1])
        g = _dot(u, wup_ref[:, d_ff + c0:d_ff + c1])
        cw = cw_ref[:, c0:c1]
        g_prev = jnp.where(keep_prev, pltpu.roll(g, 1, 0)[halo:halo + tr], 0.0)
        g_next = jnp.where(keep_next, pltpu.roll(g, ext - 1, 0)[halo:halo + tr], 0.0)
        gc = cb_ref[:, c0:c1] + g_prev * cw[0:1] + g[halo:halo + tr] * cw[1:2] + g_next * cw[2:3]
        acc = acc + _dot(_bf(jax.nn.gelu(gc) * a), wdn_ref[c0:c1, :])
    o_ref[0] = h1[halo:halo + tr] + mod[5:6] * acc


def _post(h, y1, y2, y2_col, modtab, g2, wo1, wo2, fp, row0, nrows, tr, seg, out_rows, earlier=None):
    b, l, d = h.shape
    halo = BF16_ROWS
    per = tr // halo
    base, hbase = row0 // tr, row0 // halo
    hlast = hbase + nrows // halo - 1
    yw = wo1.shape[0]
    main = lambda w, c: pl.BlockSpec((1, tr, w), lambda i, t: (i, base + t, c))
    prev = lambda w, c: pl.BlockSpec((1, halo, w), lambda i, t: (i, jnp.maximum(hbase + t * per - 1, hbase), c))
    nxt = lambda w, c: pl.BlockSpec((1, halo, w), lambda i, t: (i, jnp.minimum(hbase + (t + 1) * per, hlast), c))
    consts = [g2, wo1, wo2, fp["w_up"], fp["cw"], fp["cb"], fp["w_down"]]
    d_ff = fp["w_down"].shape[0]
    nblk = d_ff // MXU_TILE
    ff_chunks = tuple(MXU_TILE * (nblk * k // FF_CHUNKS) for k in range(FF_CHUNKS + 1))
    in_specs = [main(d, 0), prev(d, 0), nxt(d, 0),
                main(yw, 0), prev(yw, 0), nxt(yw, 0),
                main(yw, y2_col), prev(yw, y2_col), nxt(yw, y2_col),
                pl.BlockSpec((1, 1, 6, d), lambda i, t: (i, seg, 0, 0))] + [_const_spec(c.shape) for c in consts]
    args = [h, h, h, y1, y1, y1, y2, y2, y2, modtab, *consts]
    aliases = {}
    if earlier is not None:
        in_specs.append(pl.BlockSpec(memory_space=pl.ANY))
        aliases = {len(args): 0}
        args.append(earlier)
    return pl.pallas_call(
        functools.partial(_post_kernel, ff_chunks=ff_chunks),
        grid=(b, nrows // tr),
        in_specs=in_specs,
        out_specs=main(d, 0),
        out_shape=jax.ShapeDtypeStruct((b, out_rows, d), F32),
        input_output_aliases=aliases,
        compiler_params=_params(("parallel", "parallel")),
        name="post",
    )(*args)


def _rope_tables(s, l, da, blocks):
    half = da // 2
    rows = s // GRID_W
    row = jnp.repeat(jnp.arange(rows, dtype=F32), GRID_W)
    col = jnp.tile(jnp.arange(GRID_W, dtype=F32), rows)
    inv = ROPE_THETA ** (-jnp.arange(0, da, 2, dtype=F32) / da)
    ar, ac = row[:, None] * inv, col[:, None] * inv
    cr, sr, cc, sc = jnp.cos(ar), jnp.sin(ar), jnp.cos(ac), jnp.sin(ac)
    z = jnp.zeros((s, half), F32)
    cos_p, sina_p, sinb_p = [], [], []
    for blk in blocks:
        if blk[0] == "id":
            cos_p.append(jnp.ones((s, blk[1]), F32))
            sina_p.append(jnp.zeros((s, blk[1]), F32))
            sinb_p.append(jnp.zeros((s, blk[1]), F32))
        else:
            cos_p += [cr, cr, cc, cc]
            sina_p += [-sr, z, -sc, z]
            sinb_p += [z, sr, z, sc]
    pad = lambda x, v: jnp.concatenate([x, jnp.full((l - s, LANES), v, F32)], axis=0)
    return (pad(jnp.concatenate(cos_p, axis=1), 1.0), pad(jnp.concatenate(sina_p, axis=1), 0.0),
            pad(jnp.concatenate(sinb_p, axis=1), 0.0))


def _seg_matrix(seg_ids):
    ids = np.asarray(seg_ids)
    return jnp.asarray((ids[:, None] == ids[None, :]).astype(np.float32), BF16)


def _lanes(*parts):
    return jnp.concatenate([jnp.asarray(p, F32).reshape(-1) for p in parts]).reshape(1, -1)


def _even_params(e, d, w_in_ab, lru_conv_w, lru_conv_b, lru_gate_w, lru_gate_b, lru_lambda, mla_g_q,
                 mla_g_kv, mla_w_uq, mla_w_ukv, gqn, gkn, gqr, gkr, w_out_ab):
    d_rnn = lru_conv_w.shape[2]
    q_rank, kv_rank = mla_g_q.shape[1], mla_g_kv.shape[1]
    nope, rope, hh = MLA_NOPE, MLA_ROPE, MLA_HEADS
    dv = mla_w_ukv.shape[2] // hh - nope
    w_in = w_in_ab[e]
    base = 2 * d_rnn + q_rank + kv_rank
    w_in_pad = jnp.concatenate([w_in[:, :base], jnp.zeros((d, nope), F32), w_in[:, base:],
                                jnp.zeros((d, LANES - nope - rope), F32)], axis=1)
    w_uq = mla_w_uq[e].reshape(q_rank, hh, nope + rope)
    w_uq = jnp.pad(w_uq, ((0, 0), (0, 0), (0, LANES - nope - rope))).reshape(q_rank, hh * LANES)
    w_ukv = mla_w_ukv[e].reshape(kv_rank, hh, nope + dv)
    w_uk = jnp.pad(w_ukv[:, :, :nope], ((0, 0), (0, 0), (0, LANES - nope))).reshape(kv_rank, hh * LANES)
    w_uv = w_ukv[:, :, nope:].reshape(kv_rank, hh * dv)
    scale = (nope + rope) ** -0.5 * LOG2E
    rest = LANES - nope - rope
    gw = lru_gate_w[e]
    bs = gw.shape[-1]
    per = LANES // bs
    ng = d_rnn // LANES
    gwg = gw.reshape(2, 2, ng, per, bs, bs)
    eye = jnp.eye(per, dtype=F32)
    wg = jnp.einsum("dgjpcn,pq->jpcdgqn", gwg, eye).reshape(ng, LANES, 4 * LANES)
    bg = lru_gate_b[e].reshape(2, 2, ng, LANES).transpose(2, 0, 1, 3).reshape(ng, 1, 4 * LANES)
    lam = lru_lambda[e].reshape(2, ng, LANES).transpose(1, 0, 2)
    return {
        "d_rnn": d_rnn,
        "w_in": _bf(w_in_pad), "g_q": mla_g_q[e][None], "g_kv": mla_g_kv[e][None],
        "w_uq": _bf(w_uq), "w_uk": _bf(w_uk), "w_uv": _bf(w_uv),
        "m": _seg_matrix(([0] * nope + [1] * rope + [2] * rest) + ([3] * nope + [4] * rope + [5] * rest)),
        "invlen": _lanes(np.full(nope, 1.0 / nope), np.full(LANES - nope, 1.0 / rope)),
        "gainq": _lanes(gqn[e] * scale, gqr[e] * scale, np.zeros(rest)),
        "gaink": _lanes(gkn[e], np.zeros(LANES - nope)),
        "gkr": _lanes(np.zeros(nope), gkr[e], np.zeros(rest)),
        "lru_cw": lru_conv_w[e], "lru_cb": lru_conv_b[e][None],
        "lru_wg": _bf(wg), "lru_bg": bg, "lru_lam": lam,
        "wo1": _bf(w_out_ab[e][:d_rnn]), "wo2": _bf(w_out_ab[e][d_rnn:]),
    }


def _odd_params(o, d, w_in_c, diff_gq, diff_gk, diff_lam, diff_g_out, w_out_c):
    dh = diff_gq.shape[1]
    scale = dh ** -0.5 * LOG2E
    return {
        "dh": dh,
        "w_in": _bf(w_in_c[o]),
        "m": _seg_matrix([0] * dh + [1] * dh + [2] * dh + [3] * dh),
        "invlen": _lanes(np.full(LANES, 1.0 / dh)),
        "gainq": _lanes(diff_gq[o] * scale, diff_gq[o] * scale),
        "gaink": _lanes(diff_gk[o], diff_gk[o]),
        "lam_vecs": diff_lam[o], "g_out": diff_g_out[o][None],
        "wo1": _bf(w_out_c[o][:d // 2]), "wo2": _bf(w_out_c[o][d // 2:]),
    }


def kernel(x, c, ctx, c_ctx, w_mod, b_mod, g_norm1, g_norm2, w_in_ab, lru_conv_w, lru_conv_b, lru_gate_w, lru_gate_b, lru_lambda, mla_g_q, mla_g_kv, mla_w_uq, mla_w_ukv, mla_gq_nope, mla_gk_nope, mla_gq_rope, mla_gk_rope, w_out_ab, w_in_c, diff_gq, diff_gk, diff_lam, diff_g_out, w_out_c, ffn_w_up, ffn_conv_w, ffn_conv_b, ffn_w_down):
    b, s, d = x.shape
    n_ctx = ctx.shape[1]
    l = s + n_ctx
    depth = w_mod.shape[0]
    tr = ROW_TILE
    assert s % tr == 0 and n_ctx == tr and s % GRID_W == 0 and d == DIFF_HEADS * LANES
    nlat, ntiles = s // tr, l // tr

    cc = jnp.concatenate([c, c_ctx[None], jnp.zeros((BF16_ROWS - (b + 1) % BF16_ROWS, d), F32)], axis=0)
    mod = _modulation(cc, w_mod, b_mod)
    mod_lat = mod[:, :b].reshape(depth, b, 1, 6, d)
    mod_ctx = jnp.broadcast_to(mod[:, b].reshape(depth, 1, 1, 6, d), (depth, b, 1, 6, d))
    modtab = jnp.concatenate([mod_lat, mod_ctx], axis=2)

    rope_mla = _rope_tables(s, l, MLA_ROPE // 2,
                            [("id", MLA_NOPE), ("rot",), ("id", LANES - MLA_NOPE - MLA_ROPE)])
    dh = diff_gq.shape[1]
    rope_diff = _rope_tables(s, l, dh // 2, [("rot",), ("rot",)])

    def post(h, y1, y2, y2_col, modtab_l, g2, wo1, wo2, fp, last):
        out_rows = s if last else l
        out = _post(h, y1, y2, y2_col, modtab_l, g2, wo1, wo2, fp, 0, s, min(POST_TILE, s), 0, out_rows)
        if not last:
            out = _post(h, y1, y2, y2_col, modtab_l, g2, wo1, wo2, fp, s, n_ctx, tr, 1, out_rows, earlier=out)
        return out

    h = jnp.concatenate([x, ctx], axis=1)
    for li in range(depth):
        last = li == depth - 1
        out_tiles = nlat if last else ntiles
        fp = {"w_up": _bf(ffn_w_up[li]), "cw": ffn_conv_w[li], "cb": ffn_conv_b[li][None],
              "w_down": _bf(ffn_w_down[li])}
        g1, g2 = g_norm1[li][None], g_norm2[li][None]
        if li % 2 == 0:
            p = _even_params(li // 2, d, w_in_ab, lru_conv_w, lru_conv_b, lru_gate_w, lru_gate_b,
                             lru_lambda, mla_g_q, mla_g_kv, mla_w_uq, mla_w_ukv, mla_gq_nope,
                             mla_gk_nope, mla_gq_rope, mla_gk_rope, w_out_ab)
            xr, gr, q, k, vt = _even_in(h, modtab[li], g1, p, rope_mla, nlat)
            y1 = _lru(xr, gr, p, s)
            y2 = _attention(q, k, vt, s, out_tiles)
            h = post(h, y1, y2, 0, modtab[li], g2, p["wo1"], p["wo2"], fp, last)
        else:
            p = _odd_params(li // 2, d, w_in_c, diff_gq, diff_gk, diff_lam, diff_g_out, w_out_c)
            lam_init = 0.8 - 0.6 * math.exp(-0.3 * li)
            q, k, vt = _diff_in(h, modtab[li], g1, p, rope_diff, nlat)
            y = _attention(q, k, vt, s, out_tiles, (p["lam_vecs"], p["g_out"]), lam_init)
            h = post(h, y, y, 1, modtab[li], g2, p["wo1"], p["wo2"], fp, last)
    return h
```

```python
import functools
import math

import numpy as np
import jax
import jax.numpy as jnp
from jax import lax
from jax.experimental import pallas as pl
from jax.experimental.pallas import tpu as pltpu

F32 = jnp.float32
BF16 = jnp.bfloat16

EPS = 1e-6
ROPE_THETA = 10000.0
GRID_W = 64
RG_C = 8.0
LRU_BLOCKS = 8
MLA_HEADS = 8
MLA_NOPE = 64
MLA_ROPE = 32
DIFF_HEADS = 8

LANES = 128
SUBLANES = 8
MXU_TILE = 256
BF16_ROWS = 16
ROW_TILE = 256
LRU_UNROLL = 2
POST_TILE = 512
FF_CHUNKS = 4
LOG2E = 1.4426950408889634
VMEM_LIMIT = 52 * 1024 * 1024


def _bf(x):
    return x.astype(BF16)


def _dot(a, b):
    return jnp.dot(a, b, preferred_element_type=F32)


def _dot_nt(a, b):
    return lax.dot_general(a, b, (((1,), (1,)), ((), ())), preferred_element_type=F32)


def _rms(x, g):
    ms = jnp.mean(x * x, axis=-1, keepdims=True)
    return x * lax.rsqrt(ms + EPS) * g


def _expm1(x, u):
    small = jnp.where(u == 1.0, x, (u - 1.0) * x / jnp.log(u))
    return jnp.where(jnp.abs(x) > 1.0, u - 1.0, small)


def _log1p(y):
    w = 1.0 + y
    return jnp.where(w == 1.0, y, jnp.log(w) * y / (w - 1.0))


def _seg_sumsq(x, m):
    return _dot(_bf(x * x), m)


def _heads_norm(x, m, invlen, gain, nheads):
    r = x.shape[0]
    pw = m.shape[0]
    per = pw // LANES
    stacked = jnp.concatenate([x[:, p * pw:(p + 1) * pw] for p in range(nheads // per)], axis=0)
    ss = _seg_sumsq(stacked, m)
    y = stacked * lax.rsqrt(ss * jnp.tile(invlen, (1, per)) + EPS) * jnp.tile(gain, (1, per))
    return [y[(h // per) * r:(h // per + 1) * r, (h % per) * LANES:(h % per + 1) * LANES]
            for h in range(nheads)]


def _rope(x, cos, sina, sinb, shift):
    return x * cos + pltpu.roll(x, LANES - shift, 1) * sina + pltpu.roll(x, shift, 1) * sinb


def _const_spec(shape):
    n = len(shape)
    return pl.BlockSpec(shape, lambda *_: (0,) * n, pipeline_mode=pl.Buffered(1))


def _params(sem):
    return pltpu.CompilerParams(dimension_semantics=sem, vmem_limit_bytes=VMEM_LIMIT)


def _mod_kernel(cc_ref, w_ref, b_ref, o_ref):
    x = cc_ref[...]
    sx = x * jax.nn.sigmoid(x)
    o_ref[0] = _dot(_bf(sx), _bf(w_ref[0])) + b_ref[0]


def _modulation(cc, w_mod, b_mod):
    depth, d, d6 = w_mod.shape
    rows = cc.shape[0]
    return pl.pallas_call(
        _mod_kernel,
        grid=(depth, d6 // d),
        in_specs=[pl.BlockSpec((rows, d), lambda l, j: (0, 0)),
                  pl.BlockSpec((1, d, d), lambda l, j: (l, 0, j)),
                  pl.BlockSpec((1, 1, d), lambda l, j: (l, 0, j))],
        out_specs=pl.BlockSpec((1, rows, d), lambda l, j: (l, 0, j)),
        out_shape=jax.ShapeDtypeStruct((depth, rows, d6), F32),
        compiler_params=_params(("arbitrary", "arbitrary")),
        name="modulation",
    )(cc, w_mod, b_mod.reshape(depth, 1, d6))


def _even_in_kernel(h_ref, mod_ref, g1_ref, win_ref, gq_ref, gkv_ref, wuq_ref, wuk_ref, wuv_ref,
                    m_ref, invlen_ref, gainq_ref, gaink_ref, gkr_ref, cos_ref, sina_ref, sinb_ref,
                    xr_ref, gr_ref, q_ref, k_ref, vt_ref, *, d_rnn, q_rank, kv_rank):
    tr = h_ref.shape[1]
    mod = mod_ref[0, 0]
    u = _rms(h_ref[0], g1_ref[...]) * (1.0 + mod[1:2]) + mod[0:1]
    z = _dot(_bf(u), win_ref[...])
    o = 2 * d_rnn
    xr_ref[0] = z[:, :d_rnn]
    gr_ref[0] = z[:, d_rnn:o]
    ql = z[:, o:o + q_rank]
    kvl = z[:, o + q_rank:o + q_rank + kv_rank]
    kr = z[:, o + q_rank + kv_rank:]

    cos, sina, sinb = cos_ref[...], sina_ref[...], sinb_ref[...]
    m, invlen = m_ref[...], invlen_ref[...]

    q = _dot(_bf(_rms(ql, gq_ref[...])), wuq_ref[...])
    qn = _heads_norm(q, m, invlen, gainq_ref[...], MLA_HEADS)
    for h in range(MLA_HEADS):
        q_ref[0, :, h * LANES:(h + 1) * LANES] = _bf(
            _rope(qn[h], cos, sina, sinb, MLA_ROPE // 4))

    kvn = _bf(_rms(kvl, gkv_ref[...]))
    vt_ref[0, 0] = _bf(_dot(kvn, wuv_ref[...]).T)
    kn = _heads_norm(_dot(kvn, wuk_ref[...]), m, invlen, gaink_ref[...], MLA_HEADS)
    ss = jnp.sum(kr * kr, axis=-1, keepdims=True) * (1.0 / MLA_ROPE)
    k_rope = _rope(kr * lax.rsqrt(ss + EPS) * gkr_ref[...], cos, sina, sinb, MLA_ROPE // 4)
    for h in range(MLA_HEADS):
        k_ref[0, :, h * LANES:(h + 1) * LANES] = _bf(kn[h] + k_rope)


def _even_in(h, modtab, g1, p, rope, nlat):
    b, l, d = h.shape
    tr = ROW_TILE
    d_rnn = p["d_rnn"]
    hl = MLA_HEADS * LANES
    hdv = p["w_uv"].shape[1]
    row = lambda w: pl.BlockSpec((1, tr, w), lambda i, t: (i, t, 0))
    tab = pl.BlockSpec((tr, LANES), lambda i, t: (t, 0))
    consts = [g1, p["w_in"], p["g_q"], p["g_kv"], p["w_uq"], p["w_uk"], p["w_uv"],
              p["m"], p["invlen"], p["gainq"], p["gaink"], p["gkr"]]
    kern = functools.partial(_even_in_kernel, d_rnn=d_rnn, q_rank=p["g_q"].shape[1],
                             kv_rank=p["g_kv"].shape[1])
    return pl.pallas_call(
        kern,
        grid=(b, l // tr),
        in_specs=[row(d), pl.BlockSpec((1, 1, 6, d), lambda i, t: (i, t // nlat, 0, 0))]
                 + [_const_spec(c.shape) for c in consts] + [tab, tab, tab],
        out_specs=[row(d_rnn), row(d_rnn), row(hl), row(hl),
                   pl.BlockSpec((1, 1, hdv, tr), lambda i, t: (i, t, 0, 0))],
        out_shape=[jax.ShapeDtypeStruct((b, l, d_rnn), F32), jax.ShapeDtypeStruct((b, l, d_rnn), F32),
                   jax.ShapeDtypeStruct((b, l, hl), BF16), jax.ShapeDtypeStruct((b, l, hl), BF16),
                   jax.ShapeDtypeStruct((b, l // tr, hdv, tr), BF16)],
        compiler_params=_params(("parallel", "parallel")),
        name="even_in",
    )(h, modtab, *consts, *rope)


def _doubling_scan(a, b, pos, reverse):
    axis = a.ndim - 2
    n = a.shape[axis]
    d = 1
    while d < n:
        if not reverse:
            keep = pos >= d
            a_sh = jnp.where(keep, pltpu.roll(a, d, axis), 1.0)
            b_sh = jnp.where(keep, pltpu.roll(b, d, axis), 0.0)
        else:
            keep = pos < n - d
            a_sh = jnp.where(keep, pltpu.roll(a, n - d, axis), 1.0)
            b_sh = jnp.where(keep, pltpu.roll(b, n - d, axis), 0.0)
        b = a * b_sh + b
        a = a * a_sh
        d *= 2
    return a, b


def _scan_chunk(a, b, carry, reverse, grow, ab_scr):
    ch = a.shape[0]
    ngroups = ch // SUBLANES
    grouped = (ngroups, SUBLANES, LANES)
    a, b = _doubling_scan(a.reshape(grouped), b.reshape(grouped),
                          lax.broadcasted_iota(jnp.int32, grouped, 1), reverse)
    a, b = a.reshape(ch, LANES), b.reshape(ch, LANES)
    ab_scr[0] = a
    ab_scr[1] = b
    last = 0 if reverse else SUBLANES - 1
    ga = ab_scr[0, pl.ds(last, ngroups, stride=SUBLANES), :]
    gb = ab_scr[1, pl.ds(last, ngroups, stride=SUBLANES), :]
    ga, gb = _doubling_scan(ga, gb, grow, reverse)
    leaving = gb + ga * carry
    if not reverse:
        entering = jnp.where(grow >= 1, pltpu.roll(leaving, 1, 0), carry)
        carry = leaving[ngroups - 1:ngroups]
    else:
        entering = jnp.where(grow < ngroups - 1, pltpu.roll(leaving, ngroups - 1, 0), carry)
        carry = leaving[0:1]
    out = [b[g * SUBLANES:(g + 1) * SUBLANES] + a[g * SUBLANES:(g + 1) * SUBLANES] * entering[g:g + 1]
           for g in range(ngroups)]
    return jnp.concatenate(out, axis=0), carry


def _lru_kernel(xr_ref, gr_ref, cw_ref, cb_ref, wg_ref, bg_ref, lam_ref, y_ref, xpad, xc, hf, ab_scr,
                *, s, ch):
    l = xr_ref.shape[1]
    nlat, nctx = s // ch, (l - s) // ch
    conv_w = cw_ref.shape[0]
    ext_rows = ch + 16
    xpad[0:8, :] = jnp.zeros((8, LANES), F32)
    xpad[8:8 + l, :] = xr_ref[0]
    xpad[8 + l:16 + l, :] = jnp.zeros((8, LANES), F32)
    cw, cb = cw_ref[...], cb_ref[...]
    nl = -lam_ref[0]
    sp = jnp.maximum(nl, 0.0) + _log1p(jnp.exp(-jnp.abs(nl)))
    row = lax.broadcasted_iota(jnp.int32, (ch, LANES), 0)
    grow = lax.broadcasted_iota(jnp.int32, (ch // SUBLANES, LANES), 0)

    def conv(r0, in_ctx):
        ext = xpad[pl.ds(r0, ext_rows), :]
        x = jnp.zeros((ch, LANES), F32) + cb
        for w in range(conv_w):
            xs = pltpu.roll(ext, ext_rows - (7 + w), 0)[0:ch]
            src = r0 + row + (w - 1)
            if in_ctx and w == 0:
                xs = jnp.where(src >= s, xs, 0.0)
            if not in_ctx and w >= 2:
                xs = jnp.where(src < s, xs, 0.0)
            x = x + xs * cw[w:w + 1]
        return x

    def chunk(r0, direction, carry, in_ctx):
        rows = pl.ds(r0, ch)
        if direction == 0:
            x = conv(r0, in_ctx)
            xc[rows, :] = x
        else:
            x = xc[rows, :]
        c0 = direction * 2 * LANES
        pre = _dot(_bf(x), wg_ref[0, :, c0:c0 + 2 * LANES]) + bg_ref[0, :, c0:c0 + 2 * LANES]
        r = jax.nn.sigmoid(pre[:, :LANES])
        i = jax.nn.sigmoid(pre[:, LANES:])
        log_a = -RG_C * r * sp[direction:direction + 1]
        a = jnp.exp(log_a)
        var = -_expm1(2.0 * log_a, a * a)
        bb = jnp.where(var > 0.0, var * lax.rsqrt(var), 0.0) * (i * x)
        return _scan_chunk(a, bb, carry, direction == 1, grow, ab_scr)

    carry = jnp.zeros((1, LANES), F32)
    for c in range(nctx):
        r0 = s + c * ch
        hh, carry = chunk(r0, 0, carry, True)
        hf[r0:r0 + ch, :] = hh

    def fwd(i, carry):
        for u in range(LRU_UNROLL):
            r0 = pl.multiple_of((i * LRU_UNROLL + u) * ch, ch)
            hh, carry = chunk(r0, 0, carry, False)
            hf[pl.ds(r0, ch), :] = hh
        return carry

    lax.fori_loop(0, nlat // LRU_UNROLL, fwd, carry)

    def emit(r0, hb):
        rows = pl.ds(r0, ch)
        y_ref[0, rows, :] = _bf((hf[rows, :] + hb) * jax.nn.gelu(gr_ref[0, rows, :]))

    carry = jnp.zeros((1, LANES), F32)
    for c in reversed(range(nctx)):
        r0 = s + c * ch
        hb, carry = chunk(r0, 1, carry, True)
        emit(r0, hb)

    def bwd(i, carry):
        for u in range(LRU_UNROLL):
            r0 = pl.multiple_of((nlat - 1 - i * LRU_UNROLL - u) * ch, ch)
            hb, carry = chunk(r0, 1, carry, False)
            emit(r0, hb)
        return carry

    lax.fori_loop(0, nlat // LRU_UNROLL, bwd, carry)


def _lru(xr, gr, p, s):
    b, l, d_rnn = xr.shape
    ng = d_rnn // LANES
    col = pl.BlockSpec((1, l, LANES), lambda i, j: (i, 0, j))
    return pl.pallas_call(
        functools.partial(_lru_kernel, s=s, ch=ROW_TILE),
        grid=(b, ng),
        in_specs=[col, col,
                  pl.BlockSpec((p["lru_cw"].shape[0], LANES), lambda i, j: (0, j)),
                  pl.BlockSpec((1, LANES), lambda i, j: (0, j)),
                  pl.BlockSpec((1, LANES, 4 * LANES), lambda i, j: (j, 0, 0)),
                  pl.BlockSpec((1, 1, 4 * LANES), lambda i, j: (j, 0, 0)),
                  pl.BlockSpec((1, 2, LANES), lambda i, j: (j, 0, 0))],
        out_specs=col,
        out_shape=jax.ShapeDtypeStruct((b, l, d_rnn), BF16),
        scratch_shapes=[pltpu.VMEM((l + 16, LANES), F32), pltpu.VMEM((l, LANES), F32),
                        pltpu.VMEM((l, LANES), F32), pltpu.VMEM((2, ROW_TILE, LANES), F32)],
        compiler_params=_params(("parallel", "parallel")),
        name="rglru",
    )(xr, gr, p["lru_cw"], p["lru_cb"], p["lru_wg"], p["lru_bg"], p["lru_lam"])


def _attn_kernel(*refs, diff, s, lam_init):
    if diff:
        q_ref, k_ref, vt_ref, lamv_ref, gout_ref, o_ref, s_buf, e_buf = refs
    else:
        q_ref, k_ref, vt_ref, o_ref, s_buf, e_buf = refs
    tq = q_ref.shape[1]
    nchunks, hdv, kc = vt_ref.shape[1:]
    nheads = q_ref.shape[2] // LANES
    dv = hdv // nheads
    per_block = LANES // dv
    grp = kc // 8
    subs = range(2 if diff else 1)
    if diff:
        lane = lax.broadcasted_iota(jnp.int32, (tq, LANES), 1)
        lv = lamv_ref[...]
        lam = (jnp.exp(jnp.sum(lv[0:1] * lv[1:2], axis=-1, keepdims=True))
               - jnp.exp(jnp.sum(lv[2:3] * lv[3:4], axis=-1, keepdims=True)) + lam_init)
        gout = gout_ref[...]

    def q_operands(h):
        qh = q_ref[0, :, h * LANES:(h + 1) * LANES]
        if not diff:
            return [qh]
        zq = jnp.zeros_like(qh)
        first = lane < LANES // 2
        return [jnp.where(first, qh, zq), jnp.where(first, zq, qh)]

    def fold(x, op):
        return op(x.reshape(grp, 8, tq), axis=0)

    def run(j_lo, j_hi):
        mrun = lsum = None
        pend = []
        for t in range(nheads + 2):
            ha, hb, hc = t, t - 1, t - 2
            do_a, do_b, do_c = ha < nheads, 0 <= hb < nheads, 0 <= hc < nheads
            if do_a:
                qs = q_operands(ha)
            if do_c:
                ls = [jnp.sum(lsum[i], axis=0, keepdims=True) for i in subs]
                rinv = 1.0 / ls[0]
                if diff:
                    cb = jnp.broadcast_to(_bf(lam * ls[0] / ls[1]), (kc, tq))
            if do_b:
                m8 = [jnp.broadcast_to(jnp.max(mrun[i], axis=0, keepdims=True), (8, tq)) for i in subs]
            mrun = [jnp.full((8, tq), -jnp.inf, F32) for _ in subs]
            lsum = [jnp.zeros((8, tq), F32) for _ in subs]
            oacc = jnp.zeros((dv, tq), F32)
            for j in range(j_lo, j_hi):
                if do_b:
                    sb = [s_buf[i, j] for i in subs]
                if do_c:
                    eb = [e_buf[i, j] for i in subs]
                if do_a:
                    kj = k_ref[0, j * kc:(j + 1) * kc, ha * LANES:(ha + 1) * LANES]
                    sc = [_dot_nt(kj, qs[i]) for i in subs]
                    for i in subs:
                        s_buf[i, j] = sc[i]
                    mrun = [jnp.maximum(mrun[i], fold(sc[i], jnp.max)) for i in subs]
                if do_b:
                    e = [jnp.exp2(sb[i].reshape(grp, 8, tq) - m8[i]) for i in subs]
                    lsum = [lsum[i] + jnp.sum(e[i], axis=0) for i in subs]
                    for i in subs:
                        e_buf[i, j] = _bf(e[i].reshape(kc, tq))
                if do_c:
                    w = eb[0] - eb[1] * cb if diff else eb[0]
                    oacc = oacc + _dot(vt_ref[0, j, hc * dv:(hc + 1) * dv, :], w)
            if do_c:
                pend.append(oacc * rinv)
                if len(pend) == per_block:
                    o = (pend[0] if per_block == 1 else jnp.concatenate(pend, axis=0)).T
                    if diff:
                        o = _rms(o, gout) * (1.0 - lam_init)
                    c0 = (hc // per_block) * LANES
                    o_ref[0, :, c0:c0 + LANES] = _bf(o)
                    pend = []

    t = pl.program_id(1)
    nlat = s // tq

    @pl.when(t < nlat)
    def _():
        run(0, nchunks)

    @pl.when(t >= nlat)
    def _():
        run(s // kc, nchunks)


def _attention(q, k, vt, s, nq_tiles, extra=(), lam_init=None):
    b, l, hl = q.shape
    tq = ROW_TILE
    nchunks, hdv, kc = vt.shape[1:]
    diff = lam_init is not None
    nsub = 2 if diff else 1
    whole = lambda a: pl.BlockSpec((1,) + a.shape[1:], lambda i, t: (i,) + (0,) * (a.ndim - 1),
                                   pipeline_mode=pl.Buffered(1))
    return pl.pallas_call(
        functools.partial(_attn_kernel, diff=diff, s=s, lam_init=lam_init),
        grid=(b, nq_tiles),
        in_specs=[pl.BlockSpec((1, tq, hl), lambda i, t: (i, t, 0)), whole(k), whole(vt)]
                 + [_const_spec(x.shape) for x in extra],
        out_specs=pl.BlockSpec((1, tq, hdv), lambda i, t: (i, t, 0)),
        out_shape=jax.ShapeDtypeStruct((b, l, hdv), BF16),
        scratch_shapes=[pltpu.VMEM((nsub, nchunks, kc, tq), F32), pltpu.VMEM((nsub, nchunks, kc, tq), BF16)],
        compiler_params=_params(("parallel", "arbitrary")),
        name="diff_attn" if diff else "mla_attn",
    )(q, k, vt, *extra)


def _diff_in_kernel(h_ref, mod_ref, g1_ref, win_ref, m_ref, invlen_ref, gainq_ref, gaink_ref,
                    cos_ref, sina_ref, sinb_ref, q_ref, k_ref, vt_ref, *, dh):
    tr = h_ref.shape[1]
    d = h_ref.shape[2]
    mod = mod_ref[0, 0]
    u = _rms(h_ref[0], g1_ref[...]) * (1.0 + mod[1:2]) + mod[0:1]
    z = _dot(_bf(u), win_ref[...])
    vt_ref[0, 0] = _bf(z[:, 2 * d:].T)
    cos, sina, sinb = cos_ref[...], sina_ref[...], sinb_ref[...]
    m, invlen = m_ref[...], invlen_ref[...]
    for src, gain_ref, dst in ((z[:, :d], gainq_ref, q_ref), (z[:, d:2 * d], gaink_ref, k_ref)):
        xn = _heads_norm(src, m, invlen, gain_ref[...], DIFF_HEADS)
        for h in range(DIFF_HEADS):
            dst[0, :, h * LANES:(h + 1) * LANES] = _bf(
                _rope(xn[h], cos, sina, sinb, dh // 4))


def _diff_in(h, modtab, g1, p, rope, nlat):
    b, l, d = h.shape
    tr = ROW_TILE
    row = pl.BlockSpec((1, tr, d), lambda i, t: (i, t, 0))
    tab = pl.BlockSpec((tr, LANES), lambda i, t: (t, 0))
    consts = [g1, p["w_in"], p["m"], p["invlen"], p["gainq"], p["gaink"]]
    return pl.pallas_call(
        functools.partial(_diff_in_kernel, dh=p["dh"]),
        grid=(b, l // tr),
        in_specs=[row, pl.BlockSpec((1, 1, 6, d), lambda i, t: (i, t // nlat, 0, 0))]
                 + [_const_spec(c.shape) for c in consts] + [tab, tab, tab],
        out_specs=[row, row, pl.BlockSpec((1, 1, d, tr), lambda i, t: (i, t, 0, 0))],
        out_shape=[jax.ShapeDtypeStruct((b, l, d), BF16), jax.ShapeDtypeStruct((b, l, d), BF16),
                   jax.ShapeDtypeStruct((b, l // tr, d, tr), BF16)],
        compiler_params=_params(("parallel", "parallel")),
        name="diff_in",
    )(h, modtab, *consts, *rope)


def _post_kernel(hm_ref, hp_ref, hn_ref, y1m_ref, y1p_ref, y1n_ref, y2m_ref, y2p_ref, y2n_ref,
                 mod_ref, g2_ref, wo1_ref, wo2_ref, wup_ref, cw_ref, cb_ref, wdn_ref, *rest, ff_chunks):
    o_ref = rest[-1]
    tr = hm_ref.shape[1]
    halo = hp_ref.shape[1]
    ext = tr + 2 * halo
    d_ff = wdn_ref.shape[0]
    mod = mod_ref[0, 0]
    cat = lambda p, m, n: jnp.concatenate([p[0], m[0], n[0]], axis=0)
    y = _dot(cat(y1p_ref, y1m_ref, y1n_ref), wo1_ref[...]) + _dot(cat(y2p_ref, y2m_ref, y2n_ref), wo2_ref[...])
    h1 = cat(hp_ref, hm_ref, hn_ref) + mod[2:3] * y
    u = _bf(_rms(h1, g2_ref[...]) * (1.0 + mod[4:5]) + mod[3:4])

    t = pl.program_id(1)
    row = lax.broadcasted_iota(jnp.int32, (tr, 1), 0)
    keep_prev = jnp.logical_not(jnp.logical_and(t == 0, row == 0))
    keep_next = jnp.logical_not(jnp.logical_and(t == pl.num_programs(1) - 1, row == tr - 1))

    acc = jnp.zeros((tr, hm_ref.shape[2]), F32)
    for c0, c1 in zip(ff_chunks[:-1], ff_chunks[1:]):
        a = _dot(u[halo:halo + tr], wup_ref[:, c0:c1])
        g = _dot(u, wup_ref[:, d_ff + c0:d_ff + c1])
        cw = cw_ref[:, c0:c1]
        g_prev = jnp.where(keep_prev, pltpu.roll(g, 1, 0)[halo:halo + tr], 0.0)
        g_next = jnp.where(keep_next, pltpu.roll(g, ext - 1, 0)[halo:halo + tr], 0.0)
        gc = cb_ref[:, c0:c1] + g_prev * cw[0:1] + g[halo:halo + tr] * cw[1:2] + g_next * cw[2:3]
        acc = acc + _dot(_bf(jax.nn.gelu(gc) * a), wdn_ref[c0:c1, :])
    o_ref[0] = h1[halo:halo + tr] + mod[5:6] * acc


def _post(h, y1, y2, y2_col, modtab, g2, wo1, wo2, fp, row0, nrows, tr, seg, out_rows, earlier=None):
    b, l, d = h.shape
    halo = BF16_ROWS
    per = tr // halo
    base, hbase = row0 // tr, row0 // halo
    hlast = hbase + nrows // halo - 1
    yw = wo1.shape[0]
    main = lambda w, c: pl.BlockSpec((1, tr, w), lambda i, t: (i, base + t, c))
    prev = lambda w, c: pl.BlockSpec((1, halo, w), lambda i, t: (i, jnp.maximum(hbase + t * per - 1, hbase), c))
    nxt = lambda w, c: pl.BlockSpec((1, halo, w), lambda i, t: (i, jnp.minimum(hbase + (t + 1) * per, hlast), c))
    consts = [g2, wo1, wo2, fp["w_up"], fp["cw"], fp["cb"], fp["w_down"]]
    d_ff = fp["w_down"].shape[0]
    nblk = d_ff // MXU_TILE
    ff_chunks = tuple(MXU_TILE * (nblk * k // FF_CHUNKS) for k in range(FF_CHUNKS + 1))
    in_specs = [main(d, 0), prev(d, 0), nxt(d, 0),
                main(yw, 0), prev(yw, 0), nxt(yw, 0),
                main(yw, y2_col), prev(yw, y2_col), nxt(yw, y2_col),
                pl.BlockSpec((1, 1, 6, d), lambda i, t: (i, seg, 0, 0))] + [_const_spec(c.shape) for c in consts]
    args = [h, h, h, y1, y1, y1, y2, y2, y2, modtab, *consts]
    aliases = {}
    if earlier is not None:
        in_specs.append(pl.BlockSpec(memory_space=pl.ANY))
        aliases = {len(args): 0}
        args.append(earlier)
    return pl.pallas_call(
        functools.partial(_post_kernel, ff_chunks=ff_chunks),
        grid=(b, nrows // tr),
        in_specs=in_specs,
        out_specs=main(d, 0),
        out_shape=jax.ShapeDtypeStruct((b, out_rows, d), F32),
        input_output_aliases=aliases,
        compiler_params=_params(("parallel", "parallel")),
        name="post",
    )(*args)


def _rope_tables(s, l, da, blocks):
    half = da // 2
    rows = s // GRID_W
    row = jnp.repeat(jnp.arange(rows, dtype=F32), GRID_W)
    col = jnp.tile(jnp.arange(GRID_W, dtype=F32), rows)
    inv = ROPE_THETA ** (-jnp.arange(0, da, 2, dtype=F32) / da)
    ar, ac = row[:, None] * inv, col[:, None] * inv
    cr, sr, cc, sc = jnp.cos(ar), jnp.sin(ar), jnp.cos(ac), jnp.sin(ac)
    z = jnp.zeros((s, half), F32)
    cos_p, sina_p, sinb_p = [], [], []
    for blk in blocks:
        if blk[0] == "id":
            cos_p.append(jnp.ones((s, blk[1]), F32))
            sina_p.append(jnp.zeros((s, blk[1]), F32))
            sinb_p.append(jnp.zeros((s, blk[1]), F32))
        else:
            cos_p += [cr, cr, cc, cc]
            sina_p += [-sr, z, -sc, z]
            sinb_p += [z, sr, z, sc]
    pad = lambda x, v: jnp.concatenate([x, jnp.full((l - s, LANES), v, F32)], axis=0)
    return (pad(jnp.concatenate(cos_p, axis=1), 1.0), pad(jnp.concatenate(sina_p, axis=1), 0.0),
            pad(jnp.concatenate(sinb_p, axis=1), 0.0))


def _seg_matrix(seg_ids):
    ids = np.asarray(seg_ids)
    return jnp.asarray((ids[:, None] == ids[None, :]).astype(np.float32), BF16)


def _lanes(*parts):
    return jnp.concatenate([jnp.asarray(p, F32).reshape(-1) for p in parts]).reshape(1, -1)


def _even_params(e, d, w_in_ab, lru_conv_w, lru_conv_b, lru_gate_w, lru_gate_b, lru_lambda, mla_g_q,
                 mla_g_kv, mla_w_uq, mla_w_ukv, gqn, gkn, gqr, gkr, w_out_ab):
    d_rnn = lru_conv_w.shape[2]
    q_rank, kv_rank = mla_g_q.shape[1], mla_g_kv.shape[1]
    nope, rope, hh = MLA_NOPE, MLA_ROPE, MLA_HEADS
    dv = mla_w_ukv.shape[2] // hh - nope
    w_in = w_in_ab[e]
    base = 2 * d_rnn + q_rank + kv_rank
    w_in_pad = jnp.concatenate([w_in[:, :base], jnp.zeros((d, nope), F32), w_in[:, base:],
                                jnp.zeros((d, LANES - nope - rope), F32)], axis=1)
    w_uq = mla_w_uq[e].reshape(q_rank, hh, nope + rope)
    w_uq = jnp.pad(w_uq, ((0, 0), (0, 0), (0, LANES - nope - rope))).reshape(q_rank, hh * LANES)
    w_ukv = mla_w_ukv[e].reshape(kv_rank, hh, nope + dv)
    w_uk = jnp.pad(w_ukv[:, :, :nope], ((0, 0), (0, 0), (0, LANES - nope))).reshape(kv_rank, hh * LANES)
    w_uv = w_ukv[:, :, nope:].reshape(kv_rank, hh * dv)
    scale = (nope + rope) ** -0.5 * LOG2E
    rest = LANES - nope - rope
    gw = lru_gate_w[e]
    bs = gw.shape[-1]
    per = LANES // bs
    ng = d_rnn // LANES
    gwg = gw.reshape(2, 2, ng, per, bs, bs)
    eye = jnp.eye(per, dtype=F32)
    wg = jnp.einsum("dgjpcn,pq->jpcdgqn", gwg, eye).reshape(ng, LANES, 4 * LANES)
    bg = lru_gate_b[e].reshape(2, 2, ng, LANES).transpose(2, 0, 1, 3).reshape(ng, 1, 4 * LANES)
    lam = lru_lambda[e].reshape(2, ng, LANES).transpose(1, 0, 2)
    return {
        "d_rnn": d_rnn,
        "w_in": _bf(w_in_pad), "g_q": mla_g_q[e][None], "g_kv": mla_g_kv[e][None],
        "w_uq": _bf(w_uq), "w_uk": _bf(w_uk), "w_uv": _bf(w_uv),
        "m": _seg_matrix(([0] * nope + [1] * rope + [2] * rest) + ([3] * nope + [4] * rope + [5] * rest)),
        "invlen": _lanes(np.full(nope, 1.0 / nope), np.full(LANES - nope, 1.0 / rope)),
        "gainq": _lanes(gqn[e] * scale, gqr[e] * scale, np.zeros(rest)),
        "gaink": _lanes(gkn[e], np.zeros(LANES - nope)),
        "gkr": _lanes(np.zeros(nope), gkr[e], np.zeros(rest)),
        "lru_cw": lru_conv_w[e], "lru_cb": lru_conv_b[e][None],
        "lru_wg": _bf(wg), "lru_bg": bg, "lru_lam": lam,
        "wo1": _bf(w_out_ab[e][:d_rnn]), "wo2": _bf(w_out_ab[e][d_rnn:]),
    }


def _odd_params(o, d, w_in_c, diff_gq, diff_gk, diff_lam, diff_g_out, w_out_c):
    dh = diff_gq.shape[1]
    scale = dh ** -0.5 * LOG2E
    return {
        "dh": dh,
        "w_in": _bf(w_in_c[o]),
        "m": _seg_matrix([0] * dh + [1] * dh + [2] * dh + [3] * dh),
        "invlen": _lanes(np.full(LANES, 1.0 / dh)),
        "gainq": _lanes(diff_gq[o] * scale, diff_gq[o] * scale),
        "gaink": _lanes(diff_gk[o], diff_gk[o]),
        "lam_vecs": diff_lam[o], "g_out": diff_g_out[o][None],
        "wo1": _bf(w_out_c[o][:d // 2]), "wo2": _bf(w_out_c[o][d // 2:]),
    }


def kernel(x, c, ctx, c_ctx, w_mod, b_mod, g_norm1, g_norm2, w_in_ab, lru_conv_w, lru_conv_b, lru_gate_w, lru_gate_b, lru_lambda, mla_g_q, mla_g_kv, mla_w_uq, mla_w_ukv, mla_gq_nope, mla_gk_nope, mla_gq_rope, mla_gk_rope, w_out_ab, w_in_c, diff_gq, diff_gk, diff_lam, diff_g_out, w_out_c, ffn_w_up, ffn_conv_w, ffn_conv_b, ffn_w_down):
    b, s, d = x.shape
    n_ctx = ctx.shape[1]
    l = s + n_ctx
    depth = w_mod.shape[0]
    tr = ROW_TILE
    assert s % tr == 0 and n_ctx == tr and s % GRID_W == 0 and d == DIFF_HEADS * LANES
    nlat, ntiles = s // tr, l // tr

    cc = jnp.concatenate([c, c_ctx[None], jnp.zeros((BF16_ROWS - (b + 1) % BF16_ROWS, d), F32)], axis=0)
    mod = _modulation(cc, w_mod, b_mod)
    mod_lat = mod[:, :b].reshape(depth, b, 1, 6, d)
    mod_ctx = jnp.broadcast_to(mod[:, b].reshape(depth, 1, 1, 6, d), (depth, b, 1, 6, d))
    modtab = jnp.concatenate([mod_lat, mod_ctx], axis=2)

    rope_mla = _rope_tables(s, l, MLA_ROPE // 2,
                            [("id", MLA_NOPE), ("rot",), ("id", LANES - MLA_NOPE - MLA_ROPE)])
    dh = diff_gq.shape[1]
    rope_diff = _rope_tables(s, l, dh // 2, [("rot",), ("rot",)])

    def post(h, y1, y2, y2_col, modtab_l, g2, wo1, wo2, fp, last):
        out_rows = s if last else l
        out = _post(h, y1, y2, y2_col, modtab_l, g2, wo1, wo2, fp, 0, s, min(POST_TILE, s), 0, out_rows)
        if not last:
            out = _post(h, y1, y2, y2_col, modtab_l, g2, wo1, wo2, fp, s, n_ctx, tr, 1, out_rows, earlier=out)
        return out

    h = jnp.concatenate([x, ctx], axis=1)
    for li in range(depth):
        last = li == depth - 1
        out_tiles = nlat if last else ntiles
        fp = {"w_up": _bf(ffn_w_up[li]), "cw": ffn_conv_w[li], "cb": ffn_conv_b[li][None],
              "w_down": _bf(ffn_w_down[li])}
        g1, g2 = g_norm1[li][None], g_norm2[li][None]
        if li % 2 == 0:
            p = _even_params(li // 2, d, w_in_ab, lru_conv_w, lru_conv_b, lru_gate_w, lru_gate_b,
                             lru_lambda, mla_g_q, mla_g_kv, mla_w_uq, mla_w_ukv, mla_gq_nope,
                             mla_gk_nope, mla_gq_rope, mla_gk_rope, w_out_ab)
            xr, gr, q, k, vt = _even_in(h, modtab[li], g1, p, rope_mla, nlat)
            y1 = _lru(xr, gr, p, s)
            y2 = _attention(q, k, vt, s, out_tiles)
            h = post(h, y1, y2, 0, modtab[li], g2, p["wo1"], p["wo2"], fp, last)
        else:
            p = _odd_params(li // 2, d, w_in_c, diff_gq, diff_gk, diff_lam, diff_g_out, w_out_c)
            lam_init = 0.8 - 0.6 * math.exp(-0.3 * li)
            q, k, vt = _diff_in(h, modtab[li], g1, p, rope_diff, nlat)
            y = _attention(q, k, vt, s, out_tiles, (p["lam_vecs"], p["g_out"]), lam_init)
            h = post(h, y, y, 1, modtab[li], g2, p["wo1"], p["wo2"], fp, last)
    return h
```
